```python
import jax, jax.numpy as jnp
from jax import lax
import numpy as np


D_MODEL = 2048
BATCH = 4
SEQ = 4096
DEPTH = 2
DEC_BATCH = 2
DEC_SEQ = 4096
PAST_LEN = 128

HEAD_DIM = 128
EPS = 1e-6
NEG = -1e30
D_FF = 5632
ROPE_THETA = 500000.0
ROPE_DIM = HEAD_DIM // 4
N_EVEN = (DEPTH + 1) // 2
N_ODD = DEPTH // 2
A_HEADS = 8
A_BRANCHES = ((128, 1), (512, 4), (2048, 16))
B_HEADS = 8
RET_CHUNK = 128
RET_ROPE_THETA = 10000.0
C_HEADS = 4
C_DK = 128
C_DV = 256
GLA_RANK = 16
GLA_TAU = 16.0
GLA_CHUNK = 64
D_HEADS = 8
MLA_Q_RANK = 448
MLA_KV_RANK = 160
MLA_NOPE = 128
MLA_ROPE = 64
MLA_V = 128
MLA_ROPE_THETA = 10000.0
MLA_Q_BLOCK = 128

EVEN_SIZES = (A_HEADS * HEAD_DIM,) * 3 + (B_HEADS * HEAD_DIM,) * 4
ODD_SIZES = (C_HEADS * C_DK, C_HEADS * C_DK, C_HEADS * C_DV, C_HEADS * C_DV, GLA_RANK, GLA_RANK, MLA_Q_RANK, MLA_KV_RANK, MLA_ROPE)
EVEN_IN = sum(EVEN_SIZES)
ODD_IN = sum(ODD_SIZES)
EVEN_MIX = A_HEADS * HEAD_DIM + B_HEADS * HEAD_DIM
ODD_MIX = C_HEADS * C_DV + D_HEADS * MLA_V

kernel_name = 'hybrid_dilated_retention_gla_mla_encoder'


def _offsets(sizes):
    out, acc = [], 0
    for sz in sizes[:-1]:
        acc += sz
        out.append(acc)
    return out


def rmsnorm(x, g):
    xf = x.astype(jnp.float32)
    y = xf * lax.rsqrt(jnp.mean(xf * xf, axis=-1, keepdims=True) + EPS)
    return (y * g.astype(jnp.float32)).astype(x.dtype)


def swiglu(x, w_gate, w_up, w_down):
    return (jax.nn.silu(x @ w_gate) * (x @ w_up)) @ w_down


def rope(x, pos, theta, rot_dim):
    half = rot_dim // 2
    inv = jnp.power(jnp.float32(theta), -jnp.arange(half, dtype=jnp.float32) * (2.0 / rot_dim))
    ang = pos[:, None] * inv[None, :]
    cos, sin = jnp.cos(ang), jnp.sin(ang)
    xf = x.astype(jnp.float32)
    x1, x2 = xf[..., :half], xf[..., half:rot_dim]
    out = jnp.concatenate([x1 * cos - x2 * sin, x1 * sin + x2 * cos, xf[..., rot_dim:]], axis=-1)
    return out.astype(x.dtype)


def split_heads(t, n_heads):
    b, s, w = t.shape
    return t.reshape(b, s, n_heads, w // n_heads).transpose(0, 2, 1, 3)


def merge_heads(t):
    b, h, s, d = t.shape
    return t.transpose(0, 2, 1, 3).reshape(b, s, h * d)


def dilated_branch(q, k, v, window, dilation):
    b, h, s, d = q.shape
    half = window // (2 * dilation)
    L = s // dilation
    nb = -(-L // half)
    Lp = nb * half

    def to_sub(t):
        t = t.reshape(b, h, L, dilation, d).transpose(0, 1, 3, 2, 4)
        t = jnp.pad(t, ((0, 0), (0, 0), (0, 0), (0, Lp - L), (0, 0)))
        return t.reshape(b, h, dilation, nb, half, d)

    def band(t):
        tp = jnp.pad(t, ((0, 0), (0, 0), (0, 0), (1, 1), (0, 0), (0, 0)))
        return jnp.concatenate([tp[:, :, :, :-2], tp[:, :, :, 1:-1], tp[:, :, :, 2:]], axis=4)

    qs = to_sub(q)
    ks = band(to_sub(k))
    vs = band(to_sub(v)).astype(jnp.float32)
    sc = jnp.einsum('bhrnqd,bhrnkd->bhrnqk', qs, ks).astype(jnp.float32) * (d ** -0.5)
    qi = jnp.arange(nb)[:, None] * half + jnp.arange(half)[None, :]
    ki = jnp.arange(nb)[:, None] * half - half + jnp.arange(3 * half)[None, :]
    kib = ki[:, None, :]
    valid = (jnp.abs(kib - qi[:, :, None]) <= half) & (kib >= 0) & (kib < L)
    sc = jnp.where(valid, sc, NEG)
    m = jnp.max(sc, axis=-1, keepdims=True)
    p = jnp.exp(sc - m)
    den = jnp.sum(p, axis=-1, keepdims=True)
    num = jnp.einsum('bhrnqk,bhrnkd->bhrnqd', p, vs)

    def from_sub(t):
        x_dim = t.shape[-1]
        t = t.reshape(b, h, dilation, Lp, x_dim)[:, :, :, :L]
        return t.transpose(0, 1, 3, 2, 4).reshape(b, h, s, x_dim)

    return from_sub(num), from_sub(m), from_sub(den)


def dilated_attention(q, k, v):
    parts = [dilated_branch(q, k, v, w, r) for (w, r) in A_BRANCHES]
    m_all = parts[0][1]
    for part in parts[1:]:
        m_all = jnp.maximum(m_all, part[1])
    num = parts[0][0] * jnp.exp(parts[0][1] - m_all)
    den = parts[0][2] * jnp.exp(parts[0][1] - m_all)
    for part in parts[1:]:
        scale = jnp.exp(part[1] - m_all)
        num = num + part[0] * scale
        den = den + part[2] * scale
    return num / den


def retention_dir(q, k, v, log_gamma, strict):
    b, h, s, d = q.shape
    c = RET_CHUNK
    n = s // c
    qc = q.reshape(b, h, n, c, d)
    kc = k.reshape(b, h, n, c, d)
    vc = v.reshape(b, h, n, c, v.shape[-1])
    idx = jnp.arange(c, dtype=jnp.float32)
    rel = idx[:, None] - idx[None, :]
    mask = (rel > 0) if strict else (rel >= 0)
    lg = log_gamma[:, None, None]
    decay = jnp.where(mask, jnp.exp(lg * jnp.where(mask, rel, 0.0)), 0.0)
    att = jnp.einsum('bhnqd,bhnkd->bhnqk', qc, kc) * decay[None, :, None]
    out = jnp.einsum('bhnqk,bhnkd->bhnqd', att, vc)
    q_dec = jnp.exp(log_gamma[:, None] * (idx + 1.0))
    k_dec = jnp.exp(log_gamma[:, None] * (c - 1.0 - idx))
    chunk_kv = jnp.einsum('bhnkd,bhnke->nbhde', kc * k_dec[None, :, None, :, None], vc)
    chunk_decay = jnp.exp(log_gamma * c)[None, :, None, None]

    def step(state, kv):
        return chunk_decay * state + kv, state

    _, prev = lax.scan(step, jnp.zeros((b, h, d, vc.shape[-1]), jnp.float32), chunk_kv)
    out = out + jnp.einsum('bhnqd,nbhde->bhnqe', qc * q_dec[None, :, None, :, None], prev)
    return out.reshape(b, h, s, vc.shape[-1])


def mixer_even(h, w_in, w_out, ret_decay):
    b, s, _ = h.shape
    f32 = jnp.float32
    pos = jnp.arange(s, dtype=f32)
    qa, ka, va, qb, kb, vb, gb = jnp.split(h @ w_in, _offsets(EVEN_SIZES), axis=-1)
    qa = rope(split_heads(qa, A_HEADS), pos, ROPE_THETA, ROPE_DIM)
    ka = rope(split_heads(ka, A_HEADS), pos, ROPE_THETA, ROPE_DIM)
    oa = dilated_attention(qa, ka, split_heads(va, A_HEADS))
    qb = rope(split_heads(qb, B_HEADS), pos, RET_ROPE_THETA, HEAD_DIM).astype(f32)
    kb = rope(split_heads(kb, B_HEADS), pos, RET_ROPE_THETA, HEAD_DIM).astype(f32) * (HEAD_DIM ** -0.5)
    vb = split_heads(vb, B_HEADS).astype(f32)
    log_gamma = -jnp.exp(ret_decay.astype(f32))
    flip = lambda t: jnp.flip(t, axis=2)
    ob = retention_dir(qb, kb, vb, log_gamma[0], False) + flip(retention_dir(flip(qb), flip(kb), flip(vb), log_gamma[1], True))
    mu = jnp.mean(ob, axis=-1, keepdims=True)
    var = jnp.mean(jnp.square(ob - mu), axis=-1, keepdims=True)
    ob = (ob - mu) * lax.rsqrt(var + EPS)
    ob = ob * jax.nn.silu(split_heads(gb, B_HEADS).astype(f32))
    mixed = jnp.concatenate([merge_heads(oa).astype(h.dtype), merge_heads(ob).astype(h.dtype)], axis=-1)
    return mixed @ w_out


def gla_dir(q, k, v, log_a, strict):
    b, h, s, dk = q.shape
    dv = v.shape[-1]
    c = GLA_CHUNK
    n = s // c

    def chunks(t):
        return t.reshape(b, h, n, c, t.shape[-1]).transpose(2, 0, 1, 3, 4)

    idx = jnp.arange(c)
    mask = (idx[:, None] > idx[None, :]) if strict else (idx[:, None] >= idx[None, :])
    mask = mask[:, :, None]

    def step(state, inp):
        qc, kc, vc, ac = inp
        cum = jnp.cumsum(ac, axis=-2)
        last = cum[:, :, -1:, :]
        o_inter = jnp.einsum('bhtd,bhde->bhte', qc * jnp.exp(cum), state)
        diff = cum[:, :, :, None, :] - cum[:, :, None, :, :]
        gate = jnp.where(mask, jnp.exp(jnp.where(mask, diff, 0.0)), 0.0)
        att = jnp.einsum('bhtd,bhsd,bhtsd->bhts', qc, kc, gate)
        o_intra = jnp.einsum('bhts,bhse->bhte', att, vc)
        new_state = jnp.exp(last[:, :, 0, :, None]) * state + jnp.einsum('bhsd,bhse->bhde', kc * jnp.exp(last - cum), vc)
        return new_state, o_inter + o_intra

    _, out = lax.scan(step, jnp.zeros((b, h, dk, dv), jnp.float32), (chunks(q), chunks(k), chunks(v), chunks(log_a)))
    return out.transpose(1, 2, 0, 3, 4).reshape(b, h, s, dv)


def mla_attention(q_nope, q_rope, k_nope, k_rope, v):
    b, h, s, _ = q_nope.shape
    nq = s // MLA_Q_BLOCK
    scale = (MLA_NOPE + MLA_ROPE) ** -0.5

    def blocks(t):
        return t.reshape(b, h, nq, MLA_Q_BLOCK, t.shape[-1]).transpose(2, 0, 1, 3, 4)

    def attend(qs):
        qn, qr = qs
        sc = (jnp.einsum('bhqd,bhkd->bhqk', qn, k_nope) + jnp.einsum('bhqd,bkd->bhqk', qr, k_rope)).astype(jnp.float32) * scale
        p = jax.nn.softmax(sc, axis=-1)
        return jnp.einsum('bhqk,bhkd->bhqd', p.astype(v.dtype), v)

    out = lax.map(attend, (blocks(q_nope), blocks(q_rope)))
    return out.transpose(1, 2, 0, 3, 4).reshape(b, h, s, v.shape[-1])


def mixer_odd(h, w_in, w_out, gla_gate_w2, gla_gate_b, gla_norm_g, mla_q_norm_g, mla_w_uq, mla_kv_norm_g, mla_w_ukv):
    b, s, _ = h.shape
    f32 = jnp.float32
    pos = jnp.arange(s, dtype=f32)
    qc, kc, vc, rc, af, ab, cq, ckv, kr = jnp.split(h @ w_in, _offsets(ODD_SIZES), axis=-1)
    def log_gate(low, w2, bias):
        z = (low @ w2 + bias).astype(f32)
        return split_heads(jax.nn.log_sigmoid(z) / GLA_TAU, C_HEADS)
    la_f = log_gate(af, gla_gate_w2[0], gla_gate_b[0])
    la_b = log_gate(ab, gla_gate_w2[1], gla_gate_b[1])
    q = split_heads(qc, C_HEADS).astype(f32) * (C_DK ** -0.5)
    k = split_heads(kc, C_HEADS).astype(f32)
    v = split_heads(vc, C_HEADS).astype(f32)
    flip = lambda t: jnp.flip(t, axis=2)
    oc = gla_dir(q, k, v, la_f, False) + flip(gla_dir(flip(q), flip(k), flip(v), flip(la_b), True))
    oc = oc * lax.rsqrt(jnp.mean(oc * oc, axis=-1, keepdims=True) + EPS) * gla_norm_g.astype(f32)
    oc = oc * jax.nn.silu(split_heads(rc, C_HEADS).astype(f32))
    qf = split_heads(rmsnorm(cq, mla_q_norm_g) @ mla_w_uq, D_HEADS)
    q_nope = qf[..., :MLA_NOPE]
    q_rope = rope(qf[..., MLA_NOPE:], pos, MLA_ROPE_THETA, MLA_ROPE)
    kvf = split_heads(rmsnorm(ckv, mla_kv_norm_g) @ mla_w_ukv, D_HEADS)
    k_nope, v_m = kvf[..., :MLA_NOPE], kvf[..., MLA_NOPE:]
    k_rope = rope(kr, pos, MLA_ROPE_THETA, MLA_ROPE)
    od = mla_attention(q_nope, q_rope, k_nope, k_rope, v_m)
    mixed = jnp.concatenate([merge_heads(oc).astype(h.dtype), merge_heads(od).astype(h.dtype)], axis=-1)
    return mixed @ w_out


def trunk(x, norm_g, final_norm_g, ffn_w_gate, ffn_w_up, ffn_w_down, ab_w_in, ab_w_out, ret_decay,
          cd_w_in, cd_w_out, gla_gate_w2, gla_gate_b, gla_norm_g, mla_q_norm_g, mla_w_uq, mla_kv_norm_g, mla_w_ukv):
    h = x
    for i in range(DEPTH):
        j = i // 2
        h = h + 0.5 * swiglu(rmsnorm(h, norm_g[i, 0]), ffn_w_gate[i, 0], ffn_w_up[i, 0], ffn_w_down[i, 0])
        hn = rmsnorm(h, norm_g[i, 1])
        if i % 2 == 0:
            h = h + mixer_even(hn, ab_w_in[j], ab_w_out[j], ret_decay[j])
        else:
            h = h + mixer_odd(hn, cd_w_in[j], cd_w_out[j], gla_gate_w2[j], gla_gate_b[j], gla_norm_g[j],
                              mla_q_norm_g[j], mla_w_uq[j], mla_kv_norm_g[j], mla_w_ukv[j])
        h = h + 0.5 * swiglu(rmsnorm(h, norm_g[i, 2]), ffn_w_gate[i, 1], ffn_w_up[i, 1], ffn_w_down[i, 1])
    return rmsnorm(h, final_norm_g)


def setup_inputs(seed: int = 0) -> dict:
    key = jax.random.key(seed)
    ks = jax.random.split(key, 24)
    f32 = jnp.float32

    def w(k, shape, fan_in):
        return jax.random.normal(k, shape, f32) * (fan_in ** -0.5)

    def gain(k, shape):
        return 1.0 + 0.02 * jax.random.normal(k, shape, f32)

    base = np.log(-np.log1p(-np.power(2.0, -5.0 - np.arange(B_HEADS)))).astype(np.float32)
    ret_decay = jnp.asarray(base)[None, None, :] + 0.05 * jax.random.normal(ks[8], (N_EVEN, 2, B_HEADS), f32)
    return {
        'x_prompt': jax.random.normal(ks[0], (BATCH, SEQ, D_MODEL), f32),
        'x_sample': jax.random.normal(ks[1], (DEC_BATCH, DEC_SEQ, D_MODEL), f32),
        'norm_g': gain(ks[2], (DEPTH, 3, D_MODEL)),
        'final_norm_g': gain(ks[3], (D_MODEL,)),
        'ffn_w_gate': w(ks[4], (DEPTH, 2, D_MODEL, D_FF), D_MODEL),
        'ffn_w_up': w(ks[5], (DEPTH, 2, D_MODEL, D_FF), D_MODEL),
        'ffn_w_down': w(ks[6], (DEPTH, 2, D_FF, D_MODEL), D_FF),
        'ab_w_in': w(ks[7], (N_EVEN, D_MODEL, EVEN_IN), D_MODEL),
        'ab_w_out': w(ks[9], (N_EVEN, EVEN_MIX, D_MODEL), EVEN_MIX),
        'ret_decay': ret_decay,
        'cd_w_in': w(ks[10], (N_ODD, D_MODEL, ODD_IN), D_MODEL),
        'cd_w_out': w(ks[11], (N_ODD, ODD_MIX, D_MODEL), ODD_MIX),
        'gla_gate_w2': w(ks[12], (N_ODD, 2, GLA_RANK, C_HEADS * C_DK), GLA_RANK),
        'gla_gate_b': 0.1 * jax.random.normal(ks[13], (N_ODD, 2, C_HEADS * C_DK), f32),
        'gla_norm_g': gain(ks[14], (N_ODD, C_DV)),
        'mla_q_norm_g': gain(ks[15], (N_ODD, MLA_Q_RANK)),
        'mla_w_uq': w(ks[16], (N_ODD, MLA_Q_RANK, D_HEADS * (MLA_NOPE + MLA_ROPE)), MLA_Q_RANK),
        'mla_kv_norm_g': gain(ks[17], (N_ODD, MLA_KV_RANK)),
        'mla_w_ukv': w(ks[18], (N_ODD, MLA_KV_RANK, D_HEADS * (MLA_NOPE + MLA_V)), MLA_KV_RANK),
    }


def reference(x_prompt, x_sample, norm_g, final_norm_g, ffn_w_gate, ffn_w_up, ffn_w_down, ab_w_in, ab_w_out,
              ret_decay, cd_w_in, cd_w_out, gla_gate_w2, gla_gate_b, gla_norm_g, mla_q_norm_g, mla_w_uq,
              mla_kv_norm_g, mla_w_ukv):
    weights = (norm_g, final_norm_g, ffn_w_gate, ffn_w_up, ffn_w_down, ab_w_in, ab_w_out, ret_decay,
               cd_w_in, cd_w_out, gla_gate_w2, gla_gate_b, gla_norm_g, mla_q_norm_g, mla_w_uq,
               mla_kv_norm_g, mla_w_ukv)
    y_prompt = trunk(x_prompt, *weights)
    y_sample = trunk(x_sample, *weights)
    return (y_prompt, y_sample)
```

```python
import functools

import jax
import jax.numpy as jnp
from jax import lax
from jax.experimental import pallas as pl
from jax.experimental.pallas import tpu as pltpu

F32 = jnp.float32
BF16 = jnp.bfloat16

D_MODEL = 2048
D_FF = 5632
HEAD_DIM = 128
EPS = 1e-6
NEG = -1e30
ROPE_THETA = 500000.0
ROPE_DIM = HEAD_DIM // 4
A_HEADS = 8
A_BRANCHES = ((128, 1), (512, 4), (2048, 16))
B_HEADS = 8
RET_ROPE_THETA = 10000.0
C_HEADS = 4
C_DK = 128
C_DV = 256
GLA_RANK = 16
GLA_TAU = 16.0
D_HEADS = 8
MLA_Q_RANK = 448
MLA_KV_RANK = 160
MLA_NOPE = 128
MLA_ROPE = 64
MLA_V = 128
MLA_ROPE_THETA = 10000.0

EVEN_SIZES = (A_HEADS * HEAD_DIM,) * 3 + (B_HEADS * HEAD_DIM,) * 4
ODD_SIZES = (C_HEADS * C_DK, C_HEADS * C_DK, C_HEADS * C_DV, C_HEADS * C_DV, GLA_RANK, GLA_RANK,
             MLA_Q_RANK, MLA_KV_RANK, MLA_ROPE)

LANES = 128
VMEM_BYTES_V7X = 64 * 2 ** 20

ROW_TILE = 512
FF_TILE = 512
PROJ_TILE = 512
CHUNK = 128
DIL_HALF = 64
DIL_KEYS = 2 * CHUNK
MLA_Q_TILE = 256
MLA_HEAD_PAD = 256
ODD_PAD_IN = 4096


def _params(semantics, vmem_mib):
    return pltpu.CompilerParams(dimension_semantics=semantics, vmem_limit_bytes=vmem_mib * 2 ** 20)


def _rms(x):
    return x * lax.rsqrt(jnp.mean(x * x, axis=-1, keepdims=True) + EPS)


def _dot(a, b):
    return jnp.dot(a, b, preferred_element_type=F32)


def _dot_nt(a, b):
    return lax.dot_general(a, b, (((1,), (1,)), ((), ())), preferred_element_type=F32)


def _dot_tn(a, b):
    return lax.dot_general(a, b, (((0,), (0,)), ((), ())), preferred_element_type=F32)


def _silu(x):
    return x * jax.nn.sigmoid(x)


def _ffn_kernel(h_ref, g_ref, wg_ref, wu_ref, wd_ref, fg_ref, o_ref, xn_ref, *, final_norm):
    j = pl.program_id(1)

    @pl.when(j == 0)
    def _():
        xn_ref[...] = (_rms(h_ref[...]) * g_ref[...]).astype(BF16)
        o_ref[...] = jnp.zeros_like(o_ref)

    xn = xn_ref[...]
    gate = _dot(xn, wg_ref[...])
    up = _dot(xn, wu_ref[...])
    act = (_silu(gate) * up).astype(BF16)
    o_ref[...] += _dot(act, wd_ref[...])

    @pl.when(j == pl.num_programs(1) - 1)
    def _():
        y = h_ref[...] + 0.5 * o_ref[...]
        if final_norm:
            y = _rms(y) * fg_ref[...]
        o_ref[...] = y


def _ffn(h, g, w_gate, w_up, w_down, final_g=None):
    t, d = h.shape
    f = w_gate.shape[1]
    final_norm = final_g is not None
    fg = (final_g if final_norm else g).reshape(1, d)
    return pl.pallas_call(
        functools.partial(_ffn_kernel, final_norm=final_norm),
        grid=(t // ROW_TILE, f // FF_TILE),
        in_specs=[
            pl.BlockSpec((ROW_TILE, d), lambda i, j: (i, 0)),
            pl.BlockSpec((1, d), lambda i, j: (0, 0)),
            pl.BlockSpec((d, FF_TILE), lambda i, j: (0, j)),
            pl.BlockSpec((d, FF_TILE), lambda i, j: (0, j)),
            pl.BlockSpec((FF_TILE, d), lambda i, j: (j, 0)),
            pl.BlockSpec((1, d), lambda i, j: (0, 0)),
        ],
        out_specs=pl.BlockSpec((ROW_TILE, d), lambda i, j: (i, 0)),
        out_shape=jax.ShapeDtypeStruct((t, d), F32),
        scratch_shapes=[pltpu.VMEM((ROW_TILE, d), BF16)],
        compiler_params=_params(("parallel", "arbitrary"), 48),
    )(h, g.reshape(1, d), w_gate, w_up, w_down, fg)


def _norm_proj_kernel(h_ref, g_ref, w_ref, o_ref, xn_ref):
    @pl.when(pl.program_id(1) == 0)
    def _():
        xn_ref[...] = (_rms(h_ref[...]) * g_ref[...]).astype(BF16)

    o_ref[...] = _dot(xn_ref[...], w_ref[...])


def _norm_proj(h, g, w):
    t, d = h.shape
    n = w.shape[1]
    return pl.pallas_call(
        _norm_proj_kernel,
        grid=(t // ROW_TILE, n // PROJ_TILE),
        in_specs=[
            pl.BlockSpec((ROW_TILE, d), lambda i, j: (i, 0)),
            pl.BlockSpec((1, d), lambda i, j: (0, 0)),
            pl.BlockSpec((d, PROJ_TILE), lambda i, j: (0, j)),
        ],
        out_specs=pl.BlockSpec((ROW_TILE, PROJ_TILE), lambda i, j: (i, j)),
        out_shape=jax.ShapeDtypeStruct((t, n), F32),
        scratch_shapes=[pltpu.VMEM((ROW_TILE, d), BF16)],
        compiler_params=_params(("parallel", "arbitrary"), 32),
    )(h, g.reshape(1, d), w)


def _out_proj_kernel(h_ref, a_ref, b_ref, wa_ref, wb_ref, o_ref):
    o_ref[...] = (h_ref[...] + _dot(a_ref[...].astype(BF16), wa_ref[...])
                  + _dot(b_ref[...].astype(BF16), wb_ref[...]))


def _out_proj(h, a, b, wa, wb):
    t, d = h.shape
    ka, kb = a.shape[1], b.shape[1]
    return pl.pallas_call(
        _out_proj_kernel,
        grid=(t // ROW_TILE,),
        in_specs=[
            pl.BlockSpec((ROW_TILE, d), lambda i: (i, 0)),
            pl.BlockSpec((ROW_TILE, ka), lambda i: (i, 0)),
            pl.BlockSpec((ROW_TILE, kb), lambda i: (i, 0)),
            pl.BlockSpec((ka, d), lambda i: (0, 0)),
            pl.BlockSpec((kb, d), lambda i: (0, 0)),
        ],
        out_specs=pl.BlockSpec((ROW_TILE, d), lambda i: (i, 0)),
        out_shape=jax.ShapeDtypeStruct((t, d), F32),
        compiler_params=_params(("parallel",), 48),
    )(h, a, b, wa, wb)


def _rope_tables(s, theta, rot_dim, width):
    half = rot_dim // 2
    pos = jnp.arange(s, dtype=F32)
    inv = jnp.power(jnp.float32(theta), -jnp.arange(half, dtype=F32) * (2.0 / rot_dim))
    ang = pos[:, None] * inv[None, :]
    cos, sin = jnp.cos(ang), jnp.sin(ang)
    zeros = lambda n: jnp.zeros((s, n), F32)
    c = jnp.concatenate([cos, cos, jnp.ones((s, width - rot_dim), F32)], axis=1)
    s_lo = jnp.concatenate([zeros(half), sin, zeros(width - rot_dim)], axis=1)
    s_hi = jnp.concatenate([-sin, zeros(width - half)], axis=1)
    return c, s_lo, s_hi


def _rope3(x, c, s_lo, s_hi, half):
    w = x.shape[-1]
    return x * c + pltpu.roll(x, half, 1) * s_lo + pltpu.roll(x, w - half, 1) * s_hi


def _dilated_kernel(q_ref, k_ref, v_ref, c_ref, sl_ref, sh_ref, o_ref, qs_ref, ks_ref, m_ref, l_ref, *, seq):
    half = ROPE_DIM // 2
    n_chunks = seq // CHUNK
    scale = HEAD_DIM ** -0.5

    def rope_chunk(i, carry):
        rows = pl.ds(pl.multiple_of(i * CHUNK, CHUNK), CHUNK)
        c, sl, sh = c_ref[rows, :], sl_ref[rows, :], sh_ref[rows, :]
        qs_ref[rows, :] = _rope3(q_ref[rows, :], c, sl, sh, half)
        ks_ref[rows, :] = _rope3(k_ref[rows, :], c, sl, sh, half)
        return carry

    lax.fori_loop(0, n_chunks, rope_chunk, 0)

    qi_local = lax.broadcasted_iota(jnp.int32, (CHUNK, DIL_KEYS), 0)
    ki_local = lax.broadcasted_iota(jnp.int32, (CHUNK, DIL_KEYS), 1)

    def branch(dil, first):
        sub_len = seq // dil
        n_blk = sub_len // CHUNK

        def body(it, carry):
            r = it // n_blk
            q0 = (it % n_blk) * CHUNK
            k0 = jnp.clip(q0 - DIL_HALF, 0, sub_len - DIL_KEYS)
            if dil == 1:
                q_rows = pl.ds(pl.multiple_of(q0, CHUNK), CHUNK)
                k_rows = pl.ds(pl.multiple_of(k0, DIL_HALF), DIL_KEYS)
            else:
                q_rows = pl.ds(r + q0 * dil, CHUNK, stride=dil)
                k_rows = pl.ds(r + k0 * dil, DIL_KEYS, stride=dil)
            q = qs_ref[q_rows, :].astype(BF16)
            k = ks_ref[k_rows, :].astype(BF16)
            v = v_ref[k_rows, :].astype(BF16)
            sc = _dot_nt(q, k) * scale
            valid = jnp.abs((ki_local + k0) - (qi_local + q0)) <= DIL_HALF
            sc = jnp.where(valid, sc, NEG)
            m_blk = jnp.max(sc, axis=-1, keepdims=True)
            if first:
                m_new = jnp.broadcast_to(m_blk, (CHUNK, LANES))
            else:
                m_old = m_ref[q_rows, :]
                m_new = jnp.maximum(m_old, m_blk)
            p = jnp.exp(sc - jnp.concatenate([m_new, m_new], axis=1))
            l_new = jnp.sum(p, axis=-1, keepdims=True)
            o_new = _dot(p.astype(BF16), v)
            if first:
                l_new = jnp.broadcast_to(l_new, (CHUNK, LANES))
            else:
                alpha = jnp.exp(m_old - m_new)
                l_new = alpha * l_ref[q_rows, :] + l_new
                o_new = alpha * o_ref[q_rows, :] + o_new
            m_ref[q_rows, :] = m_new
            l_ref[q_rows, :] = l_new
            o_ref[q_rows, :] = o_new
            return carry

        lax.fori_loop(0, dil * n_blk, body, 0)

    for idx, (window, dil) in enumerate(A_BRANCHES):
        assert window // (2 * dil) == DIL_HALF
        branch(dil, idx == 0)

    def finish(i, carry):
        rows = pl.ds(pl.multiple_of(i * CHUNK, CHUNK), CHUNK)
        o_ref[rows, :] = o_ref[rows, :] / l_ref[rows, :]
        return carry

    lax.fori_loop(0, n_chunks, finish, 0)


def _dilated_attention(proj, tables):
    b, s, _ = proj.shape
    head = lambda off: pl.BlockSpec((None, s, HEAD_DIM), lambda bi, hi: (bi, 0, off + hi))
    table = pl.BlockSpec((s, LANES), lambda bi, hi: (0, 0))
    return pl.pallas_call(
        functools.partial(_dilated_kernel, seq=s),
        grid=(b, A_HEADS),
        in_specs=[head(0), head(A_HEADS), head(2 * A_HEADS), table, table, table],
        out_specs=pl.BlockSpec((None, s, HEAD_DIM), lambda bi, hi: (bi, 0, hi)),
        out_shape=jax.ShapeDtypeStruct((b, s, A_HEADS * HEAD_DIM), F32),
        scratch_shapes=[pltpu.VMEM((s, HEAD_DIM), F32) for _ in range(4)],
        compiler_params=_params(("parallel", "parallel"), 48),
    )(proj, proj, proj, *tables)


def _retention_kernel(dec_ref, q_ref, k_ref, v_ref, g_ref, c_ref, s_ref, o_ref,
                      kr_ref, sb_ref, sf_ref, sbc_ref, *, seq):
    hi = pl.program_id(1)
    n = seq // CHUNK
    scale = HEAD_DIM ** -0.5
    cf32 = float(CHUNK)

    lg_f = -jnp.exp(jnp.full((1, LANES), dec_ref[0, hi], F32))
    lg_b = -jnp.exp(jnp.full((1, LANES), dec_ref[1, hi], F32))
    ri = lax.broadcasted_iota(jnp.int32, (CHUNK, CHUNK), 0)
    ci = lax.broadcasted_iota(jnp.int32, (CHUNK, CHUNK), 1)
    rel = (ri - ci).astype(F32)
    decay = jnp.where(rel >= 0, jnp.exp(lg_f * jnp.maximum(rel, 0.0)), jnp.exp(lg_b * jnp.maximum(-rel, 0.0)))
    row = lax.broadcasted_iota(jnp.int32, (CHUNK, LANES), 0).astype(F32)
    q_dec_f = jnp.exp(lg_f * (row + 1.0))
    k_dec_f = jnp.exp(lg_f * (cf32 - 1.0 - row))
    q_dec_b = jnp.exp(lg_b * (cf32 - row))
    k_dec_b = jnp.exp(lg_b * row)
    chunk_dec_f = jnp.exp(lg_f * cf32)
    chunk_dec_b = jnp.exp(lg_b * cf32)

    def rope(x, rows):
        return x * c_ref[rows, :] + pltpu.roll(x, HEAD_DIM // 2, 1) * s_ref[rows, :]

    sbc_ref[...] = jnp.zeros_like(sbc_ref)

    def back(t, carry):
        c = n - 1 - t
        rows = pl.ds(pl.multiple_of(c * CHUNK, CHUNK), CHUNK)
        k = rope(k_ref[rows, :], rows) * scale
        kr_ref[rows, :] = k
        state = sbc_ref[...]
        sb_ref[c] = state.astype(BF16)
        kv = _dot_tn((k * k_dec_b).astype(BF16), v_ref[rows, :].astype(BF16))
        sbc_ref[...] = chunk_dec_b * state + kv
        return carry

    lax.fori_loop(0, n, back, 0)

    sf_ref[...] = jnp.zeros_like(sf_ref)

    def fwd(c, carry):
        rows = pl.ds(pl.multiple_of(c * CHUNK, CHUNK), CHUNK)
        q = rope(q_ref[rows, :], rows)
        k = kr_ref[rows, :]
        v = v_ref[rows, :].astype(BF16)
        att = _dot_nt(q.astype(BF16), k.astype(BF16)) * decay
        out = _dot(att.astype(BF16), v)
        state_f = sf_ref[...]
        q_both = jnp.concatenate([q * q_dec_f, q * q_dec_b], axis=1).astype(BF16)
        s_both = jnp.concatenate([state_f.astype(BF16), sb_ref[c]], axis=0)
        out = out + _dot(q_both, s_both)
        sf_ref[...] = chunk_dec_f * state_f + _dot_tn((k * k_dec_f).astype(BF16), v)
        mu = jnp.mean(out, axis=-1, keepdims=True)
        dev = out - mu
        var = jnp.mean(dev * dev, axis=-1, keepdims=True)
        o_ref[rows, :] = dev * lax.rsqrt(var + EPS) * _silu(g_ref[rows, :])
        return carry

    lax.fori_loop(0, n, fwd, 0)


def _retention(proj, ret_decay, tables):
    b, s, _ = proj.shape
    base = 3 * A_HEADS
    head = lambda off: pl.BlockSpec((None, s, HEAD_DIM), lambda bi, hi: (bi, 0, base + off + hi))
    table = pl.BlockSpec((s, LANES), lambda bi, hi: (0, 0))
    n = s // CHUNK
    return pl.pallas_call(
        functools.partial(_retention_kernel, seq=s),
        grid=(b, B_HEADS),
        in_specs=[pl.BlockSpec(memory_space=pltpu.SMEM),
                  head(0), head(B_HEADS), head(2 * B_HEADS), head(3 * B_HEADS), table, table],
        out_specs=pl.BlockSpec((None, s, HEAD_DIM), lambda bi, hi: (bi, 0, hi)),
        out_shape=jax.ShapeDtypeStruct((b, s, B_HEADS * HEAD_DIM), F32),
        scratch_shapes=[
            pltpu.VMEM((s, HEAD_DIM), F32),
            pltpu.VMEM((n, HEAD_DIM, HEAD_DIM), BF16),
            pltpu.VMEM((HEAD_DIM, HEAD_DIM), F32),
            pltpu.VMEM((HEAD_DIM, HEAD_DIM), F32),
        ],
        compiler_params=_params(("parallel", "parallel"), 48),
    )(ret_decay, proj, proj, proj, proj, *tables)


def _split3(x):
    hi = x.astype(BF16)
    r1 = x - hi.astype(F32)
    mid = r1.astype(BF16)
    lo = (r1 - mid.astype(F32)).astype(BF16)
    return hi, mid, lo


def _cumsum_dot(tri, x):
    w = x.shape[1]
    y = _dot(tri, jnp.concatenate(_split3(x), axis=1))
    return (y[:, :w] + y[:, w:2 * w]) + y[:, 2 * w:]


def _gla_kernel(q_ref, k_ref, v_ref, r_ref, low_ref, w2_ref, b2_ref, ng_ref, o_ref,
                la_ref, sb_ref, sf_ref, sbc_ref, *, seq):
    n = seq // CHUNK
    scale = C_DK ** -0.5
    mid = CHUNK // 2

    def gates(i, carry):
        rows = pl.ds(pl.multiple_of(i * ROW_TILE, ROW_TILE), ROW_TILE)
        z = _dot(low_ref[rows, :].astype(BF16), w2_ref[...]) + b2_ref[...]
        la_ref[rows, :] = (jnp.minimum(z, 0.0) - jnp.log1p(jnp.exp(-jnp.abs(z)))) * (1.0 / GLA_TAU)
        return carry

    lax.fori_loop(0, seq // ROW_TILE, gates, 0)

    ri = lax.broadcasted_iota(jnp.int32, (CHUNK, CHUNK), 0)
    ci = lax.broadcasted_iota(jnp.int32, (CHUNK, CHUNK), 1)
    lower = ci <= ri
    tri_prefix = jnp.where(lower, 1.0, 0.0).astype(BF16)
    tri_suffix = jnp.where(ci >= ri, 1.0, 0.0).astype(BF16)

    sbc_ref[...] = jnp.zeros_like(sbc_ref)

    def back(t, carry):
        c = n - 1 - t
        rows = pl.ds(pl.multiple_of(c * CHUNK, CHUNK), CHUNK)
        cb = _cumsum_dot(tri_suffix, la_ref[rows, C_DK:])
        cb0 = cb[0:1, :]
        kb = k_ref[rows, :] * jnp.exp(cb0 - cb)
        state = sbc_ref[...]
        sb_ref[c] = state.astype(BF16)
        sbc_ref[...] = state * jnp.exp(cb0) + _dot_tn(v_ref[rows, :].astype(BF16), kb.astype(BF16))
        return carry

    lax.fori_loop(0, n, back, 0)

    sf_ref[...] = jnp.zeros_like(sf_ref)

    def fwd(c, carry):
        rows = pl.ds(pl.multiple_of(c * CHUNK, CHUNK), CHUNK)
        q = q_ref[rows, :] * scale
        k = k_ref[rows, :]
        v = v_ref[rows, :].astype(BF16)
        cf = _cumsum_dot(tri_prefix, la_ref[rows, :C_DK])
        cb = _cumsum_dot(tri_suffix, la_ref[rows, C_DK:])
        cfm = cf[mid - 1:mid, :]
        cbm = cb[mid:mid + 1, :]
        a_f = _dot_nt((q * jnp.exp(cf - cfm)).astype(BF16), (k * jnp.exp(cfm - cf)).astype(BF16))
        a_b = _dot_nt((q * jnp.exp(cb - cbm)).astype(BF16), (k * jnp.exp(cbm - cb)).astype(BF16))
        att = jnp.where(lower, a_f, a_b)
        out = _dot(att.astype(BF16), v)
        state_f = sf_ref[...]
        q_both = jnp.concatenate([q * jnp.exp(cf), q * jnp.exp(cb)], axis=1).astype(BF16)
        s_both = jnp.concatenate([state_f.astype(BF16), sb_ref[c]], axis=1)
        out = out + _dot_nt(q_both, s_both)
        cfl = cf[CHUNK - 1:CHUNK, :]
        sf_ref[...] = state_f * jnp.exp(cfl) + _dot_tn(v, (k * jnp.exp(cfl - cf)).astype(BF16))
        y = _rms(out) * ng_ref[...]
        o_ref[rows, :] = y * _silu(r_ref[rows, :])
        return carry

    lax.fori_loop(0, n, fwd, 0)


def _gla(proj, w2, b2, norm_g):
    b, s, _ = proj.shape
    n = s // CHUNK
    return pl.pallas_call(
        functools.partial(_gla_kernel, seq=s),
        grid=(b, C_HEADS),
        in_specs=[
            pl.BlockSpec((None, s, C_DK), lambda bi, hi: (bi, 0, hi)),
            pl.BlockSpec((None, s, C_DK), lambda bi, hi: (bi, 0, C_HEADS + hi)),
            pl.BlockSpec((None, s, C_DV), lambda bi, hi: (bi, 0, C_HEADS + hi)),
            pl.BlockSpec((None, s, C_DV), lambda bi, hi: (bi, 0, 2 * C_HEADS + hi)),
            pl.BlockSpec((None, s, LANES), lambda bi, hi: (bi, 0, ODD_PAD_IN // LANES - 1)),
            pl.BlockSpec((None, LANES, 2 * C_DK), lambda bi, hi: (hi, 0, 0)),
            pl.BlockSpec((None, 1, 2 * C_DK), lambda bi, hi: (hi, 0, 0)),
            pl.BlockSpec((1, C_DV), lambda bi, hi: (0, 0)),
        ],
        out_specs=pl.BlockSpec((None, s, C_DV), lambda bi, hi: (bi, 0, hi)),
        out_shape=jax.ShapeDtypeStruct((b, s, C_HEADS * C_DV), F32),
        scratch_shapes=[
            pltpu.VMEM((s, 2 * C_DK), F32),
            pltpu.VMEM((n, C_DV, C_DK), BF16),
            pltpu.VMEM((C_DV, C_DK), F32),
            pltpu.VMEM((C_DV, C_DK), F32),
        ],
        compiler_params=_params(("parallel", "parallel"), 48),
    )(proj, proj, proj, proj, proj, w2, b2, norm_g.reshape(1, C_DV))


def _mla_prep_kernel(cq_ref, ckv_ref, kr_ref, gq_ref, wq_ref, gkv_ref, wkv_ref, c_ref, sl_ref, sh_ref,
                     q_ref, k_ref, v_ref):
    half = MLA_ROPE // 2
    c, sl, sh = c_ref[...], sl_ref[...], sh_ref[...]

    def norm(x, g, rank):
        ms = jnp.sum(x * x, axis=-1, keepdims=True) * (1.0 / rank)
        return (x * lax.rsqrt(ms + EPS) * g).astype(BF16)

    qf = _dot(norm(cq_ref[...], gq_ref[...], MLA_Q_RANK), wq_ref[...])
    kvf = _dot(norm(ckv_ref[...], gkv_ref[...], MLA_KV_RANK), wkv_ref[...])
    k_rope = _rope3(kr_ref[...], c, sl, sh, half).astype(BF16)
    for h in range(D_HEADS):
        lo = h * MLA_HEAD_PAD
        q_ref[:, lo:lo + MLA_NOPE] = qf[:, lo:lo + MLA_NOPE].astype(BF16)
        q_ref[:, lo + MLA_NOPE:lo + MLA_HEAD_PAD] = _rope3(
            qf[:, lo + MLA_NOPE:lo + MLA_HEAD_PAD], c, sl, sh, half).astype(BF16)
        k_ref[:, lo:lo + MLA_NOPE] = kvf[:, lo:lo + MLA_NOPE].astype(BF16)
        k_ref[:, lo + MLA_NOPE:lo + MLA_HEAD_PAD] = k_rope
        v_ref[:, h * MLA_V:(h + 1) * MLA_V] = kvf[:, lo + MLA_NOPE:lo + MLA_HEAD_PAD].astype(BF16)


def _mla_prep(proj, gq, wq, gkv, wkv, tables, seq):
    t = proj.shape[0]
    tiles_per_seq = seq // ROW_TILE
    table = pl.BlockSpec((ROW_TILE, LANES), lambda i: (i % tiles_per_seq, 0))
    const = lambda shape: pl.BlockSpec(shape, lambda i: (0, 0))
    width = D_HEADS * MLA_HEAD_PAD
    return pl.pallas_call(
        _mla_prep_kernel,
        grid=(t // ROW_TILE,),
        in_specs=[
            pl.BlockSpec((ROW_TILE, 512), lambda i: (i, 3072 // 512)),
            pl.BlockSpec((ROW_TILE, 256), lambda i: (i, 3584 // 256)),
            pl.BlockSpec((ROW_TILE, 128), lambda i: (i, 3840 // 128)),
            const((1, 512)), const((512, width)), const((1, 256)), const((256, width)),
            table, table, table,
        ],
        out_specs=[
            pl.BlockSpec((ROW_TILE, width), lambda i: (i, 0)),
            pl.BlockSpec((ROW_TILE, width), lambda i: (i, 0)),
            pl.BlockSpec((ROW_TILE, D_HEADS * MLA_V), lambda i: (i, 0)),
        ],
        out_shape=[
            jax.ShapeDtypeStruct((t, width), BF16),
            jax.ShapeDtypeStruct((t, width), BF16),
            jax.ShapeDtypeStruct((t, D_HEADS * MLA_V), BF16),
        ],
        compiler_params=_params(("parallel",), 48),
    )(proj, proj, proj, gq, wq, gkv, wkv, *tables)


def _mla_attn_kernel(q_ref, k_ref, v_ref, o_ref):
    scale = (MLA_NOPE + MLA_ROPE) ** -0.5
    sc = _dot_nt(q_ref[...], k_ref[...]) * scale
    m = jnp.max(sc, axis=-1, keepdims=True)
    p = jnp.exp(sc - m)
    den = jnp.sum(p, axis=-1, keepdims=True)
    o_ref[...] = _dot(p.astype(BF16), v_ref[...]) / den


def _mla_attention(q, k, v):
    b, s, _ = q.shape
    return pl.pallas_call(
        _mla_attn_kernel,
        grid=(b, D_HEADS, s // MLA_Q_TILE),
        in_specs=[
            pl.BlockSpec((None, MLA_Q_TILE, MLA_HEAD_PAD), lambda bi, hi, qi: (bi, qi, hi)),
            pl.BlockSpec((None, s, MLA_HEAD_PAD), lambda bi, hi, qi: (bi, 0, hi)),
            pl.BlockSpec((None, s, MLA_V), lambda bi, hi, qi: (bi, 0, hi)),
        ],
        out_specs=pl.BlockSpec((None, MLA_Q_TILE, MLA_V), lambda bi, hi, qi: (bi, qi, hi)),
        out_shape=jax.ShapeDtypeStruct((b, s, D_HEADS * MLA_V), F32),
        compiler_params=_params(("parallel", "parallel", "arbitrary"), 48),
    )(q, k, v)


def _offsets(sizes):
    out, acc = [], 0
    for sz in sizes[:-1]:
        acc += sz
        out.append(acc)
    return out


def _pad_cols(t, n):
    return jnp.pad(t, ((0, 0), (0, n - t.shape[1])))


def _odd_in_layout(w):
    qc, kc, vc, rc, af, ab, cq, ckv, kr = jnp.split(w, _offsets(ODD_SIZES), axis=1)
    cols = [qc, kc, vc, rc, _pad_cols(cq, 512), _pad_cols(ckv, 256), _pad_cols(kr, 128),
            _pad_cols(jnp.concatenate([af, ab], axis=1), 128)]
    out = jnp.concatenate(cols, axis=1)
    assert out.shape[1] == ODD_PAD_IN
    return out


def _gla_gate_layout(w2, bias):
    wf = w2[0].reshape(GLA_RANK, C_HEADS, C_DK).transpose(1, 0, 2)
    wb = w2[1].reshape(GLA_RANK, C_HEADS, C_DK).transpose(1, 0, 2)
    z = jnp.zeros_like(wf)
    top = jnp.concatenate([wf, z], axis=2)
    bot = jnp.concatenate([z, wb], axis=2)
    rest = jnp.zeros((C_HEADS, LANES - 2 * GLA_RANK, 2 * C_DK), w2.dtype)
    w = jnp.concatenate([top, bot, rest], axis=1)
    b = jnp.concatenate([bias[0].reshape(C_HEADS, 1, C_DK), bias[1].reshape(C_HEADS, 1, C_DK)], axis=2)
    return w.astype(BF16), b


def _mixer_even(h, g, w_in, w_out, ret_decay, b, s):
    proj = _norm_proj(h, g, w_in.astype(BF16)).reshape(b, s, -1)
    oa = _dilated_attention(proj, _rope_tables(s, ROPE_THETA, ROPE_DIM, HEAD_DIM))
    c, s_lo, s_hi = _rope_tables(s, RET_ROPE_THETA, HEAD_DIM, HEAD_DIM)
    ob = _retention(proj, ret_decay, (c, s_lo + s_hi))
    wo = w_out.astype(BF16)
    ka = A_HEADS * HEAD_DIM
    return _out_proj(h, oa.reshape(b * s, -1), ob.reshape(b * s, -1), wo[:ka], wo[ka:])


def _mixer_odd(h, g, w_in, w_out, gate_w2, gate_b, gla_norm_g, q_norm_g, w_uq, kv_norm_g, w_ukv, b, s):
    proj = _norm_proj(h, g, _odd_in_layout(w_in).astype(BF16))
    w2, b2 = _gla_gate_layout(gate_w2, gate_b)
    oc = _gla(proj.reshape(b, s, -1), w2, b2, gla_norm_g)
    wq = jnp.pad(w_uq.reshape(MLA_Q_RANK, D_HEADS, MLA_NOPE + MLA_ROPE),
                 ((0, 512 - MLA_Q_RANK), (0, 0), (0, MLA_HEAD_PAD - MLA_NOPE - MLA_ROPE)))
    wq = wq.reshape(512, D_HEADS * MLA_HEAD_PAD).astype(BF16)
    wkv = jnp.pad(w_ukv, ((0, 256 - MLA_KV_RANK), (0, 0))).astype(BF16)
    gq = jnp.pad(q_norm_g, (0, 512 - MLA_Q_RANK)).reshape(1, 512)
    gkv = jnp.pad(kv_norm_g, (0, 256 - MLA_KV_RANK)).reshape(1, 256)
    q, k, v = _mla_prep(proj, gq, wq, gkv, wkv, _rope_tables(s, MLA_ROPE_THETA, MLA_ROPE, LANES), s)
    od = _mla_attention(q.reshape(b, s, -1), k.reshape(b, s, -1), v.reshape(b, s, -1))
    wo = w_out.astype(BF16)
    kc = C_HEADS * C_DV
    return _out_proj(h, oc.reshape(b * s, -1), od.reshape(b * s, -1), wo[:kc], wo[kc:])


def kernel(x_prompt, x_sample, norm_g, final_norm_g, ffn_w_gate, ffn_w_up, ffn_w_down, ab_w_in, ab_w_out,
           ret_decay, cd_w_in, cd_w_out, gla_gate_w2, gla_gate_b, gla_norm_g, mla_q_norm_g, mla_w_uq,
           mla_kv_norm_g, mla_w_ukv):
    bp, s, d = x_prompt.shape
    bs = x_sample.shape[0]
    assert x_sample.shape[1:] == (s, d)
    b = bp + bs
    h = jnp.concatenate([x_prompt, x_sample], axis=0).reshape(b * s, d)
    depth = norm_g.shape[0]
    for i in range(depth):
        j = i // 2
        ffn = lambda hh, slot, fg=None: _ffn(hh, norm_g[i, 2 * slot], ffn_w_gate[i, slot].astype(BF16),
                                            ffn_w_up[i, slot].astype(BF16), ffn_w_down[i, slot].astype(BF16), fg)
        h = ffn(h, 0)
        if i % 2 == 0:
            h = _mixer_even(h, norm_g[i, 1], ab_w_in[j], ab_w_out[j], ret_decay[j], b, s)
        else:
            h = _mixer_odd(h, norm_g[i, 1], cd_w_in[j], cd_w_out[j], gla_gate_w2[j], gla_gate_b[j], gla_norm_g[j],
                           mla_q_norm_g[j], mla_w_uq[j], mla_kv_norm_g[j], mla_w_ukv[j], b, s)
        h = ffn(h, 1, final_norm_g if i == depth - 1 else None)
    y = h.reshape(b, s, d)
    return y[:bp], y[bp:]
```

```python
import functools

import jax
import jax.numpy as jnp
from jax import lax
from jax.experimental import pallas as pl
from jax.experimental.pallas import tpu as pltpu

F32 = jnp.float32
BF16 = jnp.bfloat16

D_MODEL = 2048
D_FF = 5632
HEAD_DIM = 128
EPS = 1e-6
NEG = -1e30
ROPE_THETA = 500000.0
ROPE_DIM = HEAD_DIM // 4
A_HEADS = 8
A_BRANCHES = ((128, 1), (512, 4), (2048, 16))
B_HEADS = 8
RET_ROPE_THETA = 10000.0
C_HEADS = 4
C_DK = 128
C_DV = 256
GLA_RANK = 16
GLA_TAU = 16.0
D_HEADS = 8
MLA_Q_RANK = 448
MLA_KV_RANK = 160
MLA_NOPE = 128
MLA_ROPE = 64
MLA_V = 128
MLA_ROPE_THETA = 10000.0

EVEN_SIZES = (A_HEADS * HEAD_DIM,) * 3 + (B_HEADS * HEAD_DIM,) * 4
ODD_SIZES = (C_HEADS * C_DK, C_HEADS * C_DK, C_HEADS * C_DV, C_HEADS * C_DV, GLA_RANK, GLA_RANK,
             MLA_Q_RANK, MLA_KV_RANK, MLA_ROPE)

LANES = 128
VMEM_BYTES_V7X = 64 * 2 ** 20

ROW_TILE = 512
FF_TILE = 512
PROJ_TILE = 512
PROJ_ROW_TILE = 1024
CHUNK = 128
MIXER_UNROLL = 4
DIL_UNROLL = 8
DIL_HALF = 64
DIL_KEYS = 2 * CHUNK
MLA_Q_TILE = 512
MLA_Q_SUB = 256
MLA_HEAD_PAD = 256
ODD_PAD_IN = 4096


def _params(semantics, vmem_mib):
    return pltpu.CompilerParams(dimension_semantics=semantics, vmem_limit_bytes=vmem_mib * 2 ** 20)


def _rms(x):
    return x * lax.rsqrt(jnp.mean(x * x, axis=-1, keepdims=True) + EPS)


def _dot(a, b):
    return jnp.dot(a, b, preferred_element_type=F32)


def _dot_nt(a, b):
    return lax.dot_general(a, b, (((1,), (1,)), ((), ())), preferred_element_type=F32)


def _dot_tn(a, b):
    return lax.dot_general(a, b, (((0,), (0,)), ((), ())), preferred_element_type=F32)


def _silu(x):
    return x * jax.nn.sigmoid(x)


def _ffn_kernel(h_ref, g_ref, wg_ref, wu_ref, wd_ref, fg_ref, o_ref, xn_ref, *, final_norm):
    j = pl.program_id(1)

    @pl.when(j == 0)
    def _():
        xn_ref[...] = (_rms(h_ref[...]) * g_ref[...]).astype(BF16)
        o_ref[...] = jnp.zeros_like(o_ref)

    xn = xn_ref[...]
    gate = _dot(xn, wg_ref[...])
    up = _dot(xn, wu_ref[...])
    act = (_silu(gate) * up).astype(BF16)
    o_ref[...] += _dot(act, wd_ref[...])

    @pl.when(j == pl.num_programs(1) - 1)
    def _():
        y = h_ref[...] + 0.5 * o_ref[...]
        if final_norm:
            y = _rms(y) * fg_ref[...]
        o_ref[...] = y


def _ffn(h, g, w_gate, w_up, w_down, final_g=None):
    t, d = h.shape
    f = w_gate.shape[1]
    final_norm = final_g is not None
    fg = (final_g if final_norm else g).reshape(1, d)
    return pl.pallas_call(
        functools.partial(_ffn_kernel, final_norm=final_norm),
        grid=(t // ROW_TILE, f // FF_TILE),
        in_specs=[
            pl.BlockSpec((ROW_TILE, d), lambda i, j: (i, 0)),
            pl.BlockSpec((1, d), lambda i, j: (0, 0)),
            pl.BlockSpec((d, FF_TILE), lambda i, j: (0, j)),
            pl.BlockSpec((d, FF_TILE), lambda i, j: (0, j)),
            pl.BlockSpec((FF_TILE, d), lambda i, j: (j, 0)),
            pl.BlockSpec((1, d), lambda i, j: (0, 0)),
        ],
        out_specs=pl.BlockSpec((ROW_TILE, d), lambda i, j: (i, 0)),
        out_shape=jax.ShapeDtypeStruct((t, d), F32),
        scratch_shapes=[pltpu.VMEM((ROW_TILE, d), BF16)],
        compiler_params=_params(("parallel", "arbitrary"), 48),
    )(h, g.reshape(1, d), w_gate, w_up, w_down, fg)


def _norm_proj_kernel(h_ref, g_ref, w_ref, o_ref, xn_ref):
    @pl.when(pl.program_id(1) == 0)
    def _():
        xn_ref[...] = (_rms(h_ref[...]) * g_ref[...]).astype(BF16)

    o_ref[...] = _dot(xn_ref[...], w_ref[...])


def _norm_proj(h, g, w):
    t, d = h.shape
    n = w.shape[1]
    return pl.pallas_call(
        _norm_proj_kernel,
        grid=(t // PROJ_ROW_TILE, n // PROJ_TILE),
        in_specs=[
            pl.BlockSpec((PROJ_ROW_TILE, d), lambda i, j: (i, 0)),
            pl.BlockSpec((1, d), lambda i, j: (0, 0)),
            pl.BlockSpec((d, PROJ_TILE), lambda i, j: (0, j)),
        ],
        out_specs=pl.BlockSpec((PROJ_ROW_TILE, PROJ_TILE), lambda i, j: (i, j)),
        out_shape=jax.ShapeDtypeStruct((t, n), F32),
        scratch_shapes=[pltpu.VMEM((PROJ_ROW_TILE, d), BF16)],
        compiler_params=_params(("parallel", "arbitrary"), 48),
    )(h, g.reshape(1, d), w)


def _out_proj_kernel(h_ref, a_ref, b_ref, wa_ref, wb_ref, o_ref):
    o_ref[...] = (h_ref[...] + _dot(a_ref[...].astype(BF16), wa_ref[...])
                  + _dot(b_ref[...].astype(BF16), wb_ref[...]))


def _out_proj(h, a, b, wa, wb):
    t, d = h.shape
    ka, kb = a.shape[1], b.shape[1]
    return pl.pallas_call(
        _out_proj_kernel,
        grid=(t // ROW_TILE,),
        in_specs=[
            pl.BlockSpec((ROW_TILE, d), lambda i: (i, 0)),
            pl.BlockSpec((ROW_TILE, ka), lambda i: (i, 0)),
            pl.BlockSpec((ROW_TILE, kb), lambda i: (i, 0)),
            pl.BlockSpec((ka, d), lambda i: (0, 0)),
            pl.BlockSpec((kb, d), lambda i: (0, 0)),
        ],
        out_specs=pl.BlockSpec((ROW_TILE, d), lambda i: (i, 0)),
        out_shape=jax.ShapeDtypeStruct((t, d), F32),
        compiler_params=_params(("parallel",), 48),
    )(h, a, b, wa, wb)


def _rope_tables(s, theta, rot_dim, width):
    half = rot_dim // 2
    pos = jnp.arange(s, dtype=F32)
    inv = jnp.power(jnp.float32(theta), -jnp.arange(half, dtype=F32) * (2.0 / rot_dim))
    ang = pos[:, None] * inv[None, :]
    cos, sin = jnp.cos(ang), jnp.sin(ang)
    zeros = lambda n: jnp.zeros((s, n), F32)
    c = jnp.concatenate([cos, cos, jnp.ones((s, width - rot_dim), F32)], axis=1)
    s_lo = jnp.concatenate([zeros(half), sin, zeros(width - rot_dim)], axis=1)
    s_hi = jnp.concatenate([-sin, zeros(width - half)], axis=1)
    return c, s_lo, s_hi


def _rope3(x, c, s_lo, s_hi, half):
    w = x.shape[-1]
    return x * c + pltpu.roll(x, half, 1) * s_lo + pltpu.roll(x, w - half, 1) * s_hi


def _dilated_kernel(q_ref, k_ref, v_ref, c_ref, sl_ref, sh_ref, o_ref, qs_ref, ks_ref, m_ref, l_ref, *, seq):
    half = ROPE_DIM // 2
    n_chunks = seq // CHUNK
    scale = HEAD_DIM ** -0.5

    def rope_chunk(i, carry):
        rows = pl.ds(pl.multiple_of(i * CHUNK, CHUNK), CHUNK)
        c, sl, sh = c_ref[rows, :], sl_ref[rows, :], sh_ref[rows, :]
        qs_ref[rows, :] = _rope3(q_ref[rows, :], c, sl, sh, half)
        ks_ref[rows, :] = _rope3(k_ref[rows, :], c, sl, sh, half)
        return carry

    lax.fori_loop(0, n_chunks, rope_chunk, 0)

    qi_local = lax.broadcasted_iota(jnp.int32, (CHUNK, DIL_KEYS), 0)
    ki_local = lax.broadcasted_iota(jnp.int32, (CHUNK, DIL_KEYS), 1)

    def branch(dil, first):
        sub_len = seq // dil
        n_blk = sub_len // CHUNK

        def body(it, carry):
            r = it // n_blk
            q0 = (it % n_blk) * CHUNK
            k0 = jnp.clip(q0 - DIL_HALF, 0, sub_len - DIL_KEYS)
            if dil == 1:
                q_rows = pl.ds(pl.multiple_of(q0, CHUNK), CHUNK)
                k_rows = pl.ds(pl.multiple_of(k0, DIL_HALF), DIL_KEYS)
            else:
                q_rows = pl.ds(r + q0 * dil, CHUNK, stride=dil)
                k_rows = pl.ds(r + k0 * dil, DIL_KEYS, stride=dil)
            q = qs_ref[q_rows, :].astype(BF16)
            k = ks_ref[k_rows, :].astype(BF16)
            v = v_ref[k_rows, :].astype(BF16)
            sc = _dot_nt(q, k) * scale
            valid = jnp.abs((ki_local + k0) - (qi_local + q0)) <= DIL_HALF
            sc = jnp.where(valid, sc, NEG)
            m_blk = jnp.max(sc, axis=-1, keepdims=True)
            if first:
                m_new = jnp.broadcast_to(m_blk, (CHUNK, LANES))
            else:
                m_old = m_ref[q_rows, :]
                m_new = jnp.maximum(m_old, m_blk)
            p = jnp.exp(sc - jnp.concatenate([m_new, m_new], axis=1))
            l_new = jnp.sum(p, axis=-1, keepdims=True)
            o_new = _dot(p.astype(BF16), v)
            if first:
                l_new = jnp.broadcast_to(l_new, (CHUNK, LANES))
            else:
                alpha = jnp.exp(m_old - m_new)
                l_new = alpha * l_ref[q_rows, :] + l_new
                o_new = alpha * o_ref[q_rows, :] + o_new
            m_ref[q_rows, :] = m_new
            l_ref[q_rows, :] = l_new
            o_ref[q_rows, :] = o_new
            return carry

        lax.fori_loop(0, dil * n_blk, body, 0, unroll=DIL_UNROLL if dil < 16 else DIL_UNROLL // 2)

    for idx, (window, dil) in enumerate(A_BRANCHES):
        assert window // (2 * dil) == DIL_HALF
        branch(dil, idx == 0)

    def finish(i, carry):
        rows = pl.ds(pl.multiple_of(i * CHUNK, CHUNK), CHUNK)
        o_ref[rows, :] = o_ref[rows, :] / l_ref[rows, :]
        return carry

    lax.fori_loop(0, n_chunks, finish, 0)


def _dilated_attention(proj, tables):
    b, s, _ = proj.shape
    head = lambda off: pl.BlockSpec((None, s, HEAD_DIM), lambda bi, hi: (bi, 0, off + hi))
    table = pl.BlockSpec((s, LANES), lambda bi, hi: (0, 0))
    return pl.pallas_call(
        functools.partial(_dilated_kernel, seq=s),
        grid=(b, A_HEADS),
        in_specs=[head(0), head(A_HEADS), head(2 * A_HEADS), table, table, table],
        out_specs=pl.BlockSpec((None, s, HEAD_DIM), lambda bi, hi: (bi, 0, hi)),
        out_shape=jax.ShapeDtypeStruct((b, s, A_HEADS * HEAD_DIM), F32),
        scratch_shapes=[pltpu.VMEM((s, HEAD_DIM), F32) for _ in range(4)],
        compiler_params=_params(("parallel", "parallel"), 48),
    )(proj, proj, proj, *tables)


def _retention_kernel(dec_ref, q_ref, k_ref, v_ref, g_ref, c_ref, s_ref, o_ref,
                      kr_ref, sb_ref, sf_ref, sbc_ref, *, seq):
    hi = pl.program_id(1)
    n = seq // CHUNK
    scale = HEAD_DIM ** -0.5
    cf32 = float(CHUNK)

    lg_f = -jnp.exp(jnp.full((1, LANES), dec_ref[0, hi], F32))
    lg_b = -jnp.exp(jnp.full((1, LANES), dec_ref[1, hi], F32))
    ri = lax.broadcasted_iota(jnp.int32, (CHUNK, CHUNK), 0)
    ci = lax.broadcasted_iota(jnp.int32, (CHUNK, CHUNK), 1)
    rel = (ri - ci).astype(F32)
    decay = jnp.where(rel >= 0, jnp.exp(lg_f * jnp.maximum(rel, 0.0)), jnp.exp(lg_b * jnp.maximum(-rel, 0.0)))
    row = lax.broadcasted_iota(jnp.int32, (CHUNK, LANES), 0).astype(F32)
    q_dec_f = jnp.exp(lg_f * (row + 1.0))
    k_dec_f = jnp.exp(lg_f * (cf32 - 1.0 - row))
    q_dec_b = jnp.exp(lg_b * (cf32 - row))
    k_dec_b = jnp.exp(lg_b * row)
    chunk_dec_f = jnp.exp(lg_f * cf32)
    chunk_dec_b = jnp.exp(lg_b * cf32)

    def rope(x, rows):
        return x * c_ref[rows, :] + pltpu.roll(x, HEAD_DIM // 2, 1) * s_ref[rows, :]

    sbc_ref[...] = jnp.zeros_like(sbc_ref)

    def back(t, carry):
        c = n - 1 - t
        rows = pl.ds(pl.multiple_of(c * CHUNK, CHUNK), CHUNK)
        k = rope(k_ref[rows, :], rows) * scale
        kr_ref[rows, :] = k
        state = sbc_ref[...]
        sb_ref[c] = state.astype(BF16)
        kv = _dot_tn((k * k_dec_b).astype(BF16), v_ref[rows, :].astype(BF16))
        sbc_ref[...] = chunk_dec_b * state + kv
        return carry

    lax.fori_loop(0, n, back, 0, unroll=MIXER_UNROLL)

    sf_ref[...] = jnp.zeros_like(sf_ref)

    def fwd(c, carry):
        rows = pl.ds(pl.multiple_of(c * CHUNK, CHUNK), CHUNK)
        q = rope(q_ref[rows, :], rows)
        k = kr_ref[rows, :]
        v = v_ref[rows, :].astype(BF16)
        att = _dot_nt(q.astype(BF16), k.astype(BF16)) * decay
        out = _dot(att.astype(BF16), v)
        state_f = sf_ref[...]
        q_both = jnp.concatenate([q * q_dec_f, q * q_dec_b], axis=1).astype(BF16)
        s_both = jnp.concatenate([state_f.astype(BF16), sb_ref[c]], axis=0)
        out = out + _dot(q_both, s_both)
        sf_ref[...] = chunk_dec_f * state_f + _dot_tn((k * k_dec_f).astype(BF16), v)
        mu = jnp.mean(out, axis=-1, keepdims=True)
        dev = out - mu
        var = jnp.mean(dev * dev, axis=-1, keepdims=True)
        o_ref[rows, :] = dev * lax.rsqrt(var + EPS) * _silu(g_ref[rows, :])
        return carry

    lax.fori_loop(0, n, fwd, 0, unroll=MIXER_UNROLL)


def _retention(proj, ret_decay, tables):
    b, s, _ = proj.shape
    base = 3 * A_HEADS
    head = lambda off: pl.BlockSpec((None, s, HEAD_DIM), lambda bi, hi: (bi, 0, base + off + hi))
    table = pl.BlockSpec((s, LANES), lambda bi, hi: (0, 0))
    n = s // CHUNK
    return pl.pallas_call(
        functools.partial(_retention_kernel, seq=s),
        grid=(b, B_HEADS),
        in_specs=[pl.BlockSpec(memory_space=pltpu.SMEM),
                  head(0), head(B_HEADS), head(2 * B_HEADS), head(3 * B_HEADS), table, table],
        out_specs=pl.BlockSpec((None, s, HEAD_DIM), lambda bi, hi: (bi, 0, hi)),
        out_shape=jax.ShapeDtypeStruct((b, s, B_HEADS * HEAD_DIM), F32),
        scratch_shapes=[
            pltpu.VMEM((s, HEAD_DIM), F32),
            pltpu.VMEM((n, HEAD_DIM, HEAD_DIM), BF16),
            pltpu.VMEM((HEAD_DIM, HEAD_DIM), F32),
            pltpu.VMEM((HEAD_DIM, HEAD_DIM), F32),
        ],
        compiler_params=_params(("parallel", "parallel"), 48),
    )(ret_decay, proj, proj, proj, proj, *tables)


def _split3(x):
    hi = x.astype(BF16)
    r1 = x - hi.astype(F32)
    mid = r1.astype(BF16)
    lo = (r1 - mid.astype(F32)).astype(BF16)
    return hi, mid, lo


def _cumsum_dot(tri, x):
    w = x.shape[1]
    y = _dot(tri, jnp.concatenate(_split3(x), axis=1))
    return (y[:, :w] + y[:, w:2 * w]) + y[:, 2 * w:]


def _gla_kernel(q_ref, k_ref, v_ref, r_ref, low_ref, w2_ref, b2_ref, ng_ref, o_ref,
                la_ref, sb_ref, sf_ref, sbc_ref, *, seq):
    n = seq // CHUNK
    scale = C_DK ** -0.5
    mid = CHUNK // 2

    def gates(i, carry):
        rows = pl.ds(pl.multiple_of(i * ROW_TILE, ROW_TILE), ROW_TILE)
        z = _dot(low_ref[rows, :].astype(BF16), w2_ref[...]) + b2_ref[...]
        la_ref[rows, :] = (jnp.minimum(z, 0.0) - jnp.log1p(jnp.exp(-jnp.abs(z)))) * (1.0 / GLA_TAU)
        return carry

    lax.fori_loop(0, seq // ROW_TILE, gates, 0)

    ri = lax.broadcasted_iota(jnp.int32, (CHUNK, CHUNK), 0)
    ci = lax.broadcasted_iota(jnp.int32, (CHUNK, CHUNK), 1)
    lower = ci <= ri
    tri_prefix = jnp.where(lower, 1.0, 0.0).astype(BF16)
    tri_suffix = jnp.where(ci >= ri, 1.0, 0.0).astype(BF16)

    sbc_ref[...] = jnp.zeros_like(sbc_ref)

    def back(t, carry):
        c = n - 1 - t
        rows = pl.ds(pl.multiple_of(c * CHUNK, CHUNK), CHUNK)
        cb = _cumsum_dot(tri_suffix, la_ref[rows, C_DK:])
        cb0 = cb[0:1, :]
        kb = k_ref[rows, :] * jnp.exp(cb0 - cb)
        state = sbc_ref[...]
        sb_ref[c] = state.astype(BF16)
        sbc_ref[...] = state * jnp.exp(cb0) + _dot_tn(v_ref[rows, :].astype(BF16), kb.astype(BF16))
        return carry

    lax.fori_loop(0, n, back, 0, unroll=MIXER_UNROLL)

    sf_ref[...] = jnp.zeros_like(sf_ref)

    def fwd(c, carry):
        rows = pl.ds(pl.multiple_of(c * CHUNK, CHUNK), CHUNK)
        q = q_ref[rows, :] * scale
        k = k_ref[rows, :]
        v = v_ref[rows, :].astype(BF16)
        cf = _cumsum_dot(tri_prefix, la_ref[rows, :C_DK])
        cb = _cumsum_dot(tri_suffix, la_ref[rows, C_DK:])
        cfm = cf[mid - 1:mid, :]
        cbm = cb[mid:mid + 1, :]
        a_f = _dot_nt((q * jnp.exp(cf - cfm)).astype(BF16), (k * jnp.exp(cfm - cf)).astype(BF16))
        a_b = _dot_nt((q * jnp.exp(cb - cbm)).astype(BF16), (k * jnp.exp(cbm - cb)).astype(BF16))
        att = jnp.where(lower, a_f, a_b)
        out = _dot(att.astype(BF16), v)
        state_f = sf_ref[...]
        q_both = jnp.concatenate([q * jnp.exp(cf), q * jnp.exp(cb)], axis=1).astype(BF16)
        s_both = jnp.concatenate([state_f.astype(BF16), sb_ref[c]], axis=1)
        out = out + _dot_nt(q_both, s_both)
        cfl = cf[CHUNK - 1:CHUNK, :]
        sf_ref[...] = state_f * jnp.exp(cfl) + _dot_tn(v, (k * jnp.exp(cfl - cf)).astype(BF16))
        y = _rms(out) * ng_ref[...]
        o_ref[rows, :] = y * _silu(r_ref[rows, :])
        return carry

    lax.fori_loop(0, n, fwd, 0, unroll=MIXER_UNROLL)


def _gla(proj, w2, b2, norm_g):
    b, s, _ = proj.shape
    n = s // CHUNK
    return pl.pallas_call(
        functools.partial(_gla_kernel, seq=s),
        grid=(b, C_HEADS),
        in_specs=[
            pl.BlockSpec((None, s, C_DK), lambda bi, hi: (bi, 0, hi)),
            pl.BlockSpec((None, s, C_DK), lambda bi, hi: (bi, 0, C_HEADS + hi)),
            pl.BlockSpec((None, s, C_DV), lambda bi, hi: (bi, 0, C_HEADS + hi)),
            pl.BlockSpec((None, s, C_DV), lambda bi, hi: (bi, 0, 2 * C_HEADS + hi)),
            pl.BlockSpec((None, s, LANES), lambda bi, hi: (bi, 0, ODD_PAD_IN // LANES - 1)),
            pl.BlockSpec((None, LANES, 2 * C_DK), lambda bi, hi: (hi, 0, 0)),
            pl.BlockSpec((None, 1, 2 * C_DK), lambda bi, hi: (hi, 0, 0)),
            pl.BlockSpec((1, C_DV), lambda bi, hi: (0, 0)),
        ],
        out_specs=pl.BlockSpec((None, s, C_DV), lambda bi, hi: (bi, 0, hi)),
        out_shape=jax.ShapeDtypeStruct((b, s, C_HEADS * C_DV), F32),
        scratch_shapes=[
            pltpu.VMEM((s, 2 * C_DK), F32),
            pltpu.VMEM((n, C_DV, C_DK), BF16),
            pltpu.VMEM((C_DV, C_DK), F32),
            pltpu.VMEM((C_DV, C_DK), F32),
        ],
        compiler_params=_params(("parallel", "parallel"), 48),
    )(proj, proj, proj, proj, proj, w2, b2, norm_g.reshape(1, C_DV))


def _mla_prep_kernel(cq_ref, ckv_ref, kr_ref, gq_ref, wq_ref, gkv_ref, wkv_ref, c_ref, sl_ref, sh_ref,
                     q_ref, k_ref, v_ref):
    half = MLA_ROPE // 2
    c, sl, sh = c_ref[...], sl_ref[...], sh_ref[...]

    def norm(x, g, rank):
        ms = jnp.sum(x * x, axis=-1, keepdims=True) * (1.0 / rank)
        return (x * lax.rsqrt(ms + EPS) * g).astype(BF16)

    qf = _dot(norm(cq_ref[...], gq_ref[...], MLA_Q_RANK), wq_ref[...])
    kvf = _dot(norm(ckv_ref[...], gkv_ref[...], MLA_KV_RANK), wkv_ref[...])
    k_rope = _rope3(kr_ref[...], c, sl, sh, half).astype(BF16)
    for h in range(D_HEADS):
        lo = h * MLA_HEAD_PAD
        q_ref[:, lo:lo + MLA_NOPE] = qf[:, lo:lo + MLA_NOPE].astype(BF16)
        q_ref[:, lo + MLA_NOPE:lo + MLA_HEAD_PAD] = _rope3(
            qf[:, lo + MLA_NOPE:lo + MLA_HEAD_PAD], c, sl, sh, half).astype(BF16)
        k_ref[:, lo:lo + MLA_NOPE] = kvf[:, lo:lo + MLA_NOPE].astype(BF16)
        k_ref[:, lo + MLA_NOPE:lo + MLA_HEAD_PAD] = k_rope
        v_ref[:, h * MLA_V:(h + 1) * MLA_V] = kvf[:, lo + MLA_NOPE:lo + MLA_HEAD_PAD].astype(BF16)


def _mla_prep(proj, gq, wq, gkv, wkv, tables, seq):
    t = proj.shape[0]
    tiles_per_seq = seq // ROW_TILE
    table = pl.BlockSpec((ROW_TILE, LANES), lambda i: (i % tiles_per_seq, 0))
    const = lambda shape: pl.BlockSpec(shape, lambda i: (0, 0))
    width = D_HEADS * MLA_HEAD_PAD
    return pl.pallas_call(
        _mla_prep_kernel,
        grid=(t // ROW_TILE,),
        in_specs=[
            pl.BlockSpec((ROW_TILE, 512), lambda i: (i, 3072 // 512)),
            pl.BlockSpec((ROW_TILE, 256), lambda i: (i, 3584 // 256)),
            pl.BlockSpec((ROW_TILE, 128), lambda i: (i, 3840 // 128)),
            const((1, 512)), const((512, width)), const((1, 256)), const((256, width)),
            table, table, table,
        ],
        out_specs=[
            pl.BlockSpec((ROW_TILE, width), lambda i: (i, 0)),
            pl.BlockSpec((ROW_TILE, width), lambda i: (i, 0)),
            pl.BlockSpec((ROW_TILE, D_HEADS * MLA_V), lambda i: (i, 0)),
        ],
        out_shape=[
            jax.ShapeDtypeStruct((t, width), BF16),
            jax.ShapeDtypeStruct((t, width), BF16),
            jax.ShapeDtypeStruct((t, D_HEADS * MLA_V), BF16),
        ],
        compiler_params=_params(("parallel",), 48),
    )(proj, proj, proj, gq, wq, gkv, wkv, *tables)


def _mla_attn_kernel(q_ref, k_ref, v_ref, o_ref):
    scale = (MLA_NOPE + MLA_ROPE) ** -0.5
    for sub in range(MLA_Q_TILE // MLA_Q_SUB):
        rows = slice(sub * MLA_Q_SUB, (sub + 1) * MLA_Q_SUB)
        sc = _dot_nt(q_ref[rows, :], k_ref[...]) * scale
        m = jnp.max(sc, axis=-1, keepdims=True)
        p = jnp.exp(sc - m)
        den = jnp.sum(p, axis=-1, keepdims=True)
        o_ref[rows, :] = _dot(p.astype(BF16), v_ref[...]) / den


def _mla_attention(q, k, v):
    b, s, _ = q.shape
    return pl.pallas_call(
        _mla_attn_kernel,
        grid=(b, D_HEADS, s // MLA_Q_TILE),
        in_specs=[
            pl.BlockSpec((None, MLA_Q_TILE, MLA_HEAD_PAD), lambda bi, hi, qi: (bi, qi, hi)),
            pl.BlockSpec((None, s, MLA_HEAD_PAD), lambda bi, hi, qi: (bi, 0, hi)),
            pl.BlockSpec((None, s, MLA_V), lambda bi, hi, qi: (bi, 0, hi)),
        ],
        out_specs=pl.BlockSpec((None, MLA_Q_TILE, MLA_V), lambda bi, hi, qi: (bi, qi, hi)),
        out_shape=jax.ShapeDtypeStruct((b, s, D_HEADS * MLA_V), F32),
        compiler_params=_params(("parallel", "parallel", "arbitrary"), 48),
    )(q, k, v)


def _offsets(sizes):
    out, acc = [], 0
    for sz in sizes[:-1]:
        acc += sz
        out.append(acc)
    return out


def _pad_cols(t, n):
    return jnp.pad(t, ((0, 0), (0, n - t.shape[1])))


def _odd_in_layout(w):
    qc, kc, vc, rc, af, ab, cq, ckv, kr = jnp.split(w, _offsets(ODD_SIZES), axis=1)
    cols = [qc, kc, vc, rc, _pad_cols(cq, 512), _pad_cols(ckv, 256), _pad_cols(kr, 128),
            _pad_cols(jnp.concatenate([af, ab], axis=1), 128)]
    out = jnp.concatenate(cols, axis=1)
    assert out.shape[1] == ODD_PAD_IN
    return out


def _gla_gate_layout(w2, bias):
    wf = w2[0].reshape(GLA_RANK, C_HEADS, C_DK).transpose(1, 0, 2)
    wb = w2[1].reshape(GLA_RANK, C_HEADS, C_DK).transpose(1, 0, 2)
    z = jnp.zeros_like(wf)
    top = jnp.concatenate([wf, z], axis=2)
    bot = jnp.concatenate([z, wb], axis=2)
    rest = jnp.zeros((C_HEADS, LANES - 2 * GLA_RANK, 2 * C_DK), w2.dtype)
    w = jnp.concatenate([top, bot, rest], axis=1)
    b = jnp.concatenate([bias[0].reshape(C_HEADS, 1, C_DK), bias[1].reshape(C_HEADS, 1, C_DK)], axis=2)
    return w.astype(BF16), b


def _mixer_even(h, g, w_in, w_out, ret_decay, b, s):
    proj = _norm_proj(h, g, w_in.astype(BF16)).reshape(b, s, -1)
    oa = _dilated_attention(proj, _rope_tables(s, ROPE_THETA, ROPE_DIM, HEAD_DIM))
    c, s_lo, s_hi = _rope_tables(s, RET_ROPE_THETA, HEAD_DIM, HEAD_DIM)
    ob = _retention(proj, ret_decay, (c, s_lo + s_hi))
    wo = w_out.astype(BF16)
    ka = A_HEADS * HEAD_DIM
    return _out_proj(h, oa.reshape(b * s, -1), ob.reshape(b * s, -1), wo[:ka], wo[ka:])


def _mixer_odd(h, g, w_in, w_out, gate_w2, gate_b, gla_norm_g, q_norm_g, w_uq, kv_norm_g, w_ukv, b, s):
    proj = _norm_proj(h, g, _odd_in_layout(w_in).astype(BF16))
    w2, b2 = _gla_gate_layout(gate_w2, gate_b)
    oc = _gla(proj.reshape(b, s, -1), w2, b2, gla_norm_g)
    wq = jnp.pad(w_uq.reshape(MLA_Q_RANK, D_HEADS, MLA_NOPE + MLA_ROPE),
                 ((0, 512 - MLA_Q_RANK), (0, 0), (0, MLA_HEAD_PAD - MLA_NOPE - MLA_ROPE)))
    wq = wq.reshape(512, D_HEADS * MLA_HEAD_PAD).astype(BF16)
    wkv = jnp.pad(w_ukv, ((0, 256 - MLA_KV_RANK), (0, 0))).astype(BF16)
    gq = jnp.pad(q_norm_g, (0, 512 - MLA_Q_RANK)).reshape(1, 512)
    gkv = jnp.pad(kv_norm_g, (0, 256 - MLA_KV_RANK)).reshape(1, 256)
    q, k, v = _mla_prep(proj, gq, wq, gkv, wkv, _rope_tables(s, MLA_ROPE_THETA, MLA_ROPE, LANES), s)
    od = _mla_attention(q.reshape(b, s, -1), k.reshape(b, s, -1), v.reshape(b, s, -1))
    wo = w_out.astype(BF16)
    kc = C_HEADS * C_DV
    return _out_proj(h, oc.reshape(b * s, -1), od.reshape(b * s, -1), wo[:kc], wo[kc:])


def kernel(x_prompt, x_sample, norm_g, final_norm_g, ffn_w_gate, ffn_w_up, ffn_w_down, ab_w_in, ab_w_out,
           ret_decay, cd_w_in, cd_w_out, gla_gate_w2, gla_gate_b, gla_norm_g, mla_q_norm_g, mla_w_uq,
           mla_kv_norm_g, mla_w_ukv):
    bp, s, d = x_prompt.shape
    bs = x_sample.shape[0]
    assert x_sample.shape[1:] == (s, d)
    b = bp + bs
    h = jnp.concatenate([x_prompt, x_sample], axis=0).reshape(b * s, d)
    depth = norm_g.shape[0]
    for i in range(depth):
        j = i // 2
        ffn = lambda hh, slot, fg=None: _ffn(hh, norm_g[i, 2 * slot], ffn_w_gate[i, slot].astype(BF16),
                                            ffn_w_up[i, slot].astype(BF16), ffn_w_down[i, slot].astype(BF16), fg)
        h = ffn(h, 0)
        if i % 2 == 0:
            h = _mixer_even(h, norm_g[i, 1], ab_w_in[j], ab_w_out[j], ret_decay[j], b, s)
        else:
            h = _mixer_odd(h, norm_g[i, 1], cd_w_in[j], cd_w_out[j], gla_gate_w2[j], gla_gate_b[j], gla_norm_g[j],
                           mla_q_norm_g[j], mla_w_uq[j], mla_kv_norm_g[j], mla_w_ukv[j], b, s)
        h = ffn(h, 1, final_norm_g if i == depth - 1 else None)
    y = h.reshape(b, s, d)
    return y[:bp], y[bp:]
```

```python
import functools

import jax
import jax.numpy as jnp
from jax import lax
from jax.experimental import pallas as pl
from jax.experimental.pallas import tpu as pltpu

F32 = jnp.float32
BF16 = jnp.bfloat16

D_MODEL = 2048
D_FF = 5632
HEAD_DIM = 128
EPS = 1e-6
NEG = -1e30
ROPE_THETA = 500000.0
ROPE_DIM = HEAD_DIM // 4
A_HEADS = 8
A_BRANCHES = ((128, 1), (512, 4), (2048, 16))
B_HEADS = 8
RET_ROPE_THETA = 10000.0
C_HEADS = 4
C_DK = 128
C_DV = 256
GLA_RANK = 16
GLA_TAU = 16.0
D_HEADS = 8
MLA_Q_RANK = 448
MLA_KV_RANK = 160
MLA_NOPE = 128
MLA_ROPE = 64
MLA_V = 128
MLA_ROPE_THETA = 10000.0

EVEN_SIZES = (A_HEADS * HEAD_DIM,) * 3 + (B_HEADS * HEAD_DIM,) * 4
ODD_SIZES = (C_HEADS * C_DK, C_HEADS * C_DK, C_HEADS * C_DV, C_HEADS * C_DV, GLA_RANK, GLA_RANK,
             MLA_Q_RANK, MLA_KV_RANK, MLA_ROPE)

LANES = 128
VMEM_BYTES_V7X = 64 * 2 ** 20

ROW_TILE = 512
FF_TILE = 512
PROJ_TILE = 512
PROJ_ROW_TILE = 1024
CHUNK = 128
MIXER_UNROLL = 8
DIL_UNROLL = 8
DIL_HALF = 64
DIL_KEYS = 2 * CHUNK
MLA_Q_TILE = 512
MLA_Q_SUB = 256
MLA_HEAD_PAD = 256
ODD_PAD_IN = 4096


def _params(semantics, vmem_mib):
    return pltpu.CompilerParams(dimension_semantics=semantics, vmem_limit_bytes=vmem_mib * 2 ** 20)


def _rms(x):
    return x * lax.rsqrt(jnp.mean(x * x, axis=-1, keepdims=True) + EPS)


def _dot(a, b):
    return jnp.dot(a, b, preferred_element_type=F32)


def _dot_nt(a, b):
    return lax.dot_general(a, b, (((1,), (1,)), ((), ())), preferred_element_type=F32)


def _dot_tn(a, b):
    return lax.dot_general(a, b, (((0,), (0,)), ((), ())), preferred_element_type=F32)


def _silu(x):
    return x * jax.nn.sigmoid(x)


def _ffn_kernel(h_ref, g_ref, wg_ref, wu_ref, wd_ref, fg_ref, o_ref, xn_ref, *, final_norm):
    j = pl.program_id(1)

    @pl.when(j == 0)
    def _():
        xn_ref[...] = (_rms(h_ref[...]) * g_ref[...]).astype(BF16)
        o_ref[...] = jnp.zeros_like(o_ref)

    xn = xn_ref[...]
    gate = _dot(xn, wg_ref[...])
    up = _dot(xn, wu_ref[...])
    act = (_silu(gate) * up).astype(BF16)
    o_ref[...] += _dot(act, wd_ref[...])

    @pl.when(j == pl.num_programs(1) - 1)
    def _():
        y = h_ref[...] + 0.5 * o_ref[...]
        if final_norm:
            y = _rms(y) * fg_ref[...]
        o_ref[...] = y


def _ffn(h, g, w_gate, w_up, w_down, final_g=None):
    t, d = h.shape
    f = w_gate.shape[1]
    final_norm = final_g is not None
    fg = (final_g if final_norm else g).reshape(1, d)
    return pl.pallas_call(
        functools.partial(_ffn_kernel, final_norm=final_norm),
        grid=(t // ROW_TILE, f // FF_TILE),
        in_specs=[
            pl.BlockSpec((ROW_TILE, d), lambda i, j: (i, 0)),
            pl.BlockSpec((1, d), lambda i, j: (0, 0)),
            pl.BlockSpec((d, FF_TILE), lambda i, j: (0, j)),
            pl.BlockSpec((d, FF_TILE), lambda i, j: (0, j)),
            pl.BlockSpec((FF_TILE, d), lambda i, j: (j, 0)),
            pl.BlockSpec((1, d), lambda i, j: (0, 0)),
        ],
        out_specs=pl.BlockSpec((ROW_TILE, d), lambda i, j: (i, 0)),
        out_shape=jax.ShapeDtypeStruct((t, d), F32),
        scratch_shapes=[pltpu.VMEM((ROW_TILE, d), BF16)],
        compiler_params=_params(("parallel", "arbitrary"), 48),
    )(h, g.reshape(1, d), w_gate, w_up, w_down, fg)


def _norm_proj_kernel(h_ref, g_ref, w_ref, o_ref, xn_ref):
    @pl.when(pl.program_id(1) == 0)
    def _():
        xn_ref[...] = (_rms(h_ref[...]) * g_ref[...]).astype(BF16)

    o_ref[...] = _dot(xn_ref[...], w_ref[...])


def _norm_proj(h, g, w):
    t, d = h.shape
    n = w.shape[1]
    return pl.pallas_call(
        _norm_proj_kernel,
        grid=(t // PROJ_ROW_TILE, n // PROJ_TILE),
        in_specs=[
            pl.BlockSpec((PROJ_ROW_TILE, d), lambda i, j: (i, 0)),
            pl.BlockSpec((1, d), lambda i, j: (0, 0)),
            pl.BlockSpec((d, PROJ_TILE), lambda i, j: (0, j)),
        ],
        out_specs=pl.BlockSpec((PROJ_ROW_TILE, PROJ_TILE), lambda i, j: (i, j)),
        out_shape=jax.ShapeDtypeStruct((t, n), F32),
        scratch_shapes=[pltpu.VMEM((PROJ_ROW_TILE, d), BF16)],
        compiler_params=_params(("parallel", "arbitrary"), 48),
    )(h, g.reshape(1, d), w)


def _out_proj_kernel(h_ref, a_ref, b_ref, wa_ref, wb_ref, o_ref):
    o_ref[...] = (h_ref[...] + _dot(a_ref[...].astype(BF16), wa_ref[...])
                  + _dot(b_ref[...].astype(BF16), wb_ref[...]))


def _out_proj(h, a, b, wa, wb):
    t, d = h.shape
    ka, kb = a.shape[1], b.shape[1]
    return pl.pallas_call(
        _out_proj_kernel,
        grid=(t // ROW_TILE,),
        in_specs=[
            pl.BlockSpec((ROW_TILE, d), lambda i: (i, 0)),
            pl.BlockSpec((ROW_TILE, ka), lambda i: (i, 0)),
            pl.BlockSpec((ROW_TILE, kb), lambda i: (i, 0)),
            pl.BlockSpec((ka, d), lambda i: (0, 0)),
            pl.BlockSpec((kb, d), lambda i: (0, 0)),
        ],
        out_specs=pl.BlockSpec((ROW_TILE, d), lambda i: (i, 0)),
        out_shape=jax.ShapeDtypeStruct((t, d), F32),
        compiler_params=_params(("parallel",), 48),
    )(h, a, b, wa, wb)


def _rope_tables(s, theta, rot_dim, width):
    half = rot_dim // 2
    pos = jnp.arange(s, dtype=F32)
    inv = jnp.power(jnp.float32(theta), -jnp.arange(half, dtype=F32) * (2.0 / rot_dim))
    ang = pos[:, None] * inv[None, :]
    cos, sin = jnp.cos(ang), jnp.sin(ang)
    zeros = lambda n: jnp.zeros((s, n), F32)
    c = jnp.concatenate([cos, cos, jnp.ones((s, width - rot_dim), F32)], axis=1)
    s_lo = jnp.concatenate([zeros(half), sin, zeros(width - rot_dim)], axis=1)
    s_hi = jnp.concatenate([-sin, zeros(width - half)], axis=1)
    return c, s_lo, s_hi


def _rope3(x, c, s_lo, s_hi, half):
    w = x.shape[-1]
    return x * c + pltpu.roll(x, half, 1) * s_lo + pltpu.roll(x, w - half, 1) * s_hi


def _dilated_kernel(q_ref, k_ref, v_ref, c_ref, sl_ref, sh_ref, o_ref, qs_ref, ks_ref, m_ref, l_ref, *, seq):
    half = ROPE_DIM // 2
    n_chunks = seq // CHUNK
    scale = HEAD_DIM ** -0.5

    def rope_chunk(i, carry):
        rows = pl.ds(pl.multiple_of(i * CHUNK, CHUNK), CHUNK)
        c, sl, sh = c_ref[rows, :], sl_ref[rows, :], sh_ref[rows, :]
        qs_ref[rows, :] = _rope3(q_ref[rows, :], c, sl, sh, half)
        ks_ref[rows, :] = _rope3(k_ref[rows, :], c, sl, sh, half)
        return carry

    lax.fori_loop(0, n_chunks, rope_chunk, 0)

    qi_local = lax.broadcasted_iota(jnp.int32, (CHUNK, DIL_KEYS), 0)
    ki_local = lax.broadcasted_iota(jnp.int32, (CHUNK, DIL_KEYS), 1)

    def branch(dil, first):
        sub_len = seq // dil
        n_blk = sub_len // CHUNK

        def body(it, carry):
            r = it // n_blk
            q0 = (it % n_blk) * CHUNK
            k0 = jnp.clip(q0 - DIL_HALF, 0, sub_len - DIL_KEYS)
            if dil == 1:
                q_rows = pl.ds(pl.multiple_of(q0, CHUNK), CHUNK)
                k_rows = pl.ds(pl.multiple_of(k0, DIL_HALF), DIL_KEYS)
            else:
                q_rows = pl.ds(r + q0 * dil, CHUNK, stride=dil)
                k_rows = pl.ds(r + k0 * dil, DIL_KEYS, stride=dil)
            q = qs_ref[q_rows, :].astype(BF16)
            k = ks_ref[k_rows, :].astype(BF16)
            v = v_ref[k_rows, :].astype(BF16)
            sc = _dot_nt(q, k) * scale
            valid = jnp.abs((ki_local + k0) - (qi_local + q0)) <= DIL_HALF
            sc = jnp.where(valid, sc, NEG)
            m_blk = jnp.max(sc, axis=-1, keepdims=True)
            if first:
                m_new = jnp.broadcast_to(m_blk, (CHUNK, LANES))
            else:
                m_old = m_ref[q_rows, :]
                m_new = jnp.maximum(m_old, m_blk)
            p = jnp.exp(sc - jnp.concatenate([m_new, m_new], axis=1))
            l_new = jnp.sum(p, axis=-1, keepdims=True)
            o_new = _dot(p.astype(BF16), v)
            if first:
                l_new = jnp.broadcast_to(l_new, (CHUNK, LANES))
            else:
                alpha = jnp.exp(m_old - m_new)
                l_new = alpha * l_ref[q_rows, :] + l_new
                o_new = alpha * o_ref[q_rows, :] + o_new
            m_ref[q_rows, :] = m_new
            l_ref[q_rows, :] = l_new
            o_ref[q_rows, :] = o_new
            return carry

        lax.fori_loop(0, dil * n_blk, body, 0, unroll=DIL_UNROLL if dil < 16 else DIL_UNROLL // 2)

    for idx, (window, dil) in enumerate(A_BRANCHES):
        assert window // (2 * dil) == DIL_HALF
        branch(dil, idx == 0)

    def finish(i, carry):
        rows = pl.ds(pl.multiple_of(i * CHUNK, CHUNK), CHUNK)
        o_ref[rows, :] = o_ref[rows, :] / l_ref[rows, :]
        return carry

    lax.fori_loop(0, n_chunks, finish, 0)


def _dilated_attention(proj, tables):
    b, s, _ = proj.shape
    head = lambda off: pl.BlockSpec((None, s, HEAD_DIM), lambda bi, hi: (bi, 0, off + hi))
    table = pl.BlockSpec((s, LANES), lambda bi, hi: (0, 0))
    return pl.pallas_call(
        functools.partial(_dilated_kernel, seq=s),
        grid=(b, A_HEADS),
        in_specs=[head(0), head(A_HEADS), head(2 * A_HEADS), table, table, table],
        out_specs=pl.BlockSpec((None, s, HEAD_DIM), lambda bi, hi: (bi, 0, hi)),
        out_shape=jax.ShapeDtypeStruct((b, s, A_HEADS * HEAD_DIM), F32),
        scratch_shapes=[pltpu.VMEM((s, HEAD_DIM), F32) for _ in range(4)],
        compiler_params=_params(("parallel", "parallel"), 48),
    )(proj, proj, proj, *tables)


def _retention_kernel(dec_ref, q_ref, k_ref, v_ref, g_ref, c_ref, s_ref, o_ref,
                      kr_ref, sb_ref, sf_ref, sbc_ref, *, seq):
    hi = pl.program_id(1)
    n = seq // CHUNK
    scale = HEAD_DIM ** -0.5
    cf32 = float(CHUNK)

    lg_f = -jnp.exp(jnp.full((1, LANES), dec_ref[0, hi], F32))
    lg_b = -jnp.exp(jnp.full((1, LANES), dec_ref[1, hi], F32))
    ri = lax.broadcasted_iota(jnp.int32, (CHUNK, CHUNK), 0)
    ci = lax.broadcasted_iota(jnp.int32, (CHUNK, CHUNK), 1)
    rel = (ri - ci).astype(F32)
    decay = jnp.where(rel >= 0, jnp.exp(lg_f * jnp.maximum(rel, 0.0)), jnp.exp(lg_b * jnp.maximum(-rel, 0.0)))
    row = lax.broadcasted_iota(jnp.int32, (CHUNK, LANES), 0).astype(F32)
    q_dec_f = jnp.exp(lg_f * (row + 1.0))
    k_dec_f = jnp.exp(lg_f * (cf32 - 1.0 - row))
    q_dec_b = jnp.exp(lg_b * (cf32 - row))
    k_dec_b = jnp.exp(lg_b * row)
    chunk_dec_f = jnp.exp(lg_f * cf32)
    chunk_dec_b = jnp.exp(lg_b * cf32)

    def rope(x, rows):
        return x * c_ref[rows, :] + pltpu.roll(x, HEAD_DIM // 2, 1) * s_ref[rows, :]

    sbc_ref[...] = jnp.zeros_like(sbc_ref)

    def back(t, carry):
        c = n - 1 - t
        rows = pl.ds(pl.multiple_of(c * CHUNK, CHUNK), CHUNK)
        k = rope(k_ref[rows, :], rows) * scale
        kr_ref[rows, :] = k
        state = sbc_ref[...]
        sb_ref[c] = state.astype(BF16)
        kv = _dot_tn((k * k_dec_b).astype(BF16), v_ref[rows, :].astype(BF16))
        sbc_ref[...] = chunk_dec_b * state + kv
        return carry

    lax.fori_loop(0, n, back, 0, unroll=MIXER_UNROLL)

    sf_ref[...] = jnp.zeros_like(sf_ref)

    def fwd(c, carry):
        rows = pl.ds(pl.multiple_of(c * CHUNK, CHUNK), CHUNK)
        q = rope(q_ref[rows, :], rows)
        k = kr_ref[rows, :]
        v = v_ref[rows, :].astype(BF16)
        att = _dot_nt(q.astype(BF16), k.astype(BF16)) * decay
        out = _dot(att.astype(BF16), v)
        state_f = sf_ref[...]
        q_both = jnp.concatenate([q * q_dec_f, q * q_dec_b], axis=1).astype(BF16)
        s_both = jnp.concatenate([state_f.astype(BF16), sb_ref[c]], axis=0)
        out = out + _dot(q_both, s_both)
        sf_ref[...] = chunk_dec_f * state_f + _dot_tn((k * k_dec_f).astype(BF16), v)
        mu = jnp.mean(out, axis=-1, keepdims=True)
        dev = out - mu
        var = jnp.mean(dev * dev, axis=-1, keepdims=True)
        o_ref[rows, :] = dev * lax.rsqrt(var + EPS) * _silu(g_ref[rows, :])
        return carry

    lax.fori_loop(0, n, fwd, 0, unroll=MIXER_UNROLL)


def _retention(proj, ret_decay, tables):
    b, s, _ = proj.shape
    base = 3 * A_HEADS
    head = lambda off: pl.BlockSpec((None, s, HEAD_DIM), lambda bi, hi: (bi, 0, base + off + hi))
    table = pl.BlockSpec((s, LANES), lambda bi, hi: (0, 0))
    n = s // CHUNK
    return pl.pallas_call(
        functools.partial(_retention_kernel, seq=s),
        grid=(b, B_HEADS),
        in_specs=[pl.BlockSpec(memory_space=pltpu.SMEM),
                  head(0), head(B_HEADS), head(2 * B_HEADS), head(3 * B_HEADS), table, table],
        out_specs=pl.BlockSpec((None, s, HEAD_DIM), lambda bi, hi: (bi, 0, hi)),
        out_shape=jax.ShapeDtypeStruct((b, s, B_HEADS * HEAD_DIM), F32),
        scratch_shapes=[
            pltpu.VMEM((s, HEAD_DIM), F32),
            pltpu.VMEM((n, HEAD_DIM, HEAD_DIM), BF16),
            pltpu.VMEM((HEAD_DIM, HEAD_DIM), F32),
            pltpu.VMEM((HEAD_DIM, HEAD_DIM), F32),
        ],
        compiler_params=_params(("parallel", "parallel"), 48),
    )(ret_decay, proj, proj, proj, proj, *tables)


def _split3(x):
    hi = x.astype(BF16)
    r1 = x - hi.astype(F32)
    mid = r1.astype(BF16)
    lo = (r1 - mid.astype(F32)).astype(BF16)
    return hi, mid, lo


def _cumsum_dot(tri, x):
    w = x.shape[1]
    y = _dot(tri, jnp.concatenate(_split3(x), axis=1))
    return (y[:, :w] + y[:, w:2 * w]) + y[:, 2 * w:]


def _gla_kernel(q_ref, k_ref, v_ref, r_ref, low_ref, w2_ref, b2_ref, ng_ref, o_ref,
                la_ref, sb_ref, sf_ref, sbc_ref, *, seq):
    n = seq // CHUNK
    scale = C_DK ** -0.5
    mid = CHUNK // 2

    def gates(i, carry):
        rows = pl.ds(pl.multiple_of(i * ROW_TILE, ROW_TILE), ROW_TILE)
        z = _dot(low_ref[rows, :].astype(BF16), w2_ref[...]) + b2_ref[...]
        la_ref[rows, :] = (jnp.minimum(z, 0.0) - jnp.log1p(jnp.exp(-jnp.abs(z)))) * (1.0 / GLA_TAU)
        return carry

    lax.fori_loop(0, seq // ROW_TILE, gates, 0)

    ri = lax.broadcasted_iota(jnp.int32, (CHUNK, CHUNK), 0)
    ci = lax.broadcasted_iota(jnp.int32, (CHUNK, CHUNK), 1)
    lower = ci <= ri
    tri_prefix = jnp.where(lower, 1.0, 0.0).astype(BF16)
    tri_suffix = jnp.where(ci >= ri, 1.0, 0.0).astype(BF16)

    sbc_ref[...] = jnp.zeros_like(sbc_ref)

    def back(t, carry):
        c = n - 1 - t
        rows = pl.ds(pl.multiple_of(c * CHUNK, CHUNK), CHUNK)
        cb = _cumsum_dot(tri_suffix, la_ref[rows, C_DK:])
        cb0 = cb[0:1, :]
        kb = k_ref[rows, :] * jnp.exp(cb0 - cb)
        state = sbc_ref[...]
        sb_ref[c] = state.astype(BF16)
        sbc_ref[...] = state * jnp.exp(cb0) + _dot_tn(v_ref[rows, :].astype(BF16), kb.astype(BF16))
        return carry

    lax.fori_loop(0, n, back, 0, unroll=MIXER_UNROLL)

    sf_ref[...] = jnp.zeros_like(sf_ref)

    def fwd(c, carry):
        rows = pl.ds(pl.multiple_of(c * CHUNK, CHUNK), CHUNK)
        q = q_ref[rows, :] * scale
        k = k_ref[rows, :]
        v = v_ref[rows, :].astype(BF16)
        cf = _cumsum_dot(tri_prefix, la_ref[rows, :C_DK])
        cb = _cumsum_dot(tri_suffix, la_ref[rows, C_DK:])
        cfm = cf[mid - 1:mid, :]
        cbm = cb[mid:mid + 1, :]
        a_f = _dot_nt((q * jnp.exp(cf - cfm)).astype(BF16), (k * jnp.exp(cfm - cf)).astype(BF16))
        a_b = _dot_nt((q * jnp.exp(cb - cbm)).astype(BF16), (k * jnp.exp(cbm - cb)).astype(BF16))
        att = jnp.where(lower, a_f, a_b)
        out = _dot(att.astype(BF16), v)
        state_f = sf_ref[...]
        q_both = jnp.concatenate([q * jnp.exp(cf), q * jnp.exp(cb)], axis=1).astype(BF16)
        s_both = jnp.concatenate([state_f.astype(BF16), sb_ref[c]], axis=1)
        out = out + _dot_nt(q_both, s_both)
        cfl = cf[CHUNK - 1:CHUNK, :]
        sf_ref[...] = state_f * jnp.exp(cfl) + _dot_tn(v, (k * jnp.exp(cfl - cf)).astype(BF16))
        y = _rms(out) * ng_ref[...]
        o_ref[rows, :] = y * _silu(r_ref[rows, :])
        return carry

    lax.fori_loop(0, n, fwd, 0, unroll=MIXER_UNROLL)


def _gla(proj, w2, b2, norm_g):
    b, s, _ = proj.shape
    n = s // CHUNK
    return pl.pallas_call(
        functools.partial(_gla_kernel, seq=s),
        grid=(b, C_HEADS),
        in_specs=[
            pl.BlockSpec((None, s, C_DK), lambda bi, hi: (bi, 0, hi)),
            pl.BlockSpec((None, s, C_DK), lambda bi, hi: (bi, 0, C_HEADS + hi)),
            pl.BlockSpec((None, s, C_DV), lambda bi, hi: (bi, 0, C_HEADS + hi)),
            pl.BlockSpec((None, s, C_DV), lambda bi, hi: (bi, 0, 2 * C_HEADS + hi)),
            pl.BlockSpec((None, s, LANES), lambda bi, hi: (bi, 0, ODD_PAD_IN // LANES - 1)),
            pl.BlockSpec((None, LANES, 2 * C_DK), lambda bi, hi: (hi, 0, 0)),
            pl.BlockSpec((None, 1, 2 * C_DK), lambda bi, hi: (hi, 0, 0)),
            pl.BlockSpec((1, C_DV), lambda bi, hi: (0, 0)),
        ],
        out_specs=pl.BlockSpec((None, s, C_DV), lambda bi, hi: (bi, 0, hi)),
        out_shape=jax.ShapeDtypeStruct((b, s, C_HEADS * C_DV), F32),
        scratch_shapes=[
            pltpu.VMEM((s, 2 * C_DK), F32),
            pltpu.VMEM((n, C_DV, C_DK), BF16),
            pltpu.VMEM((C_DV, C_DK), F32),
            pltpu.VMEM((C_DV, C_DK), F32),
        ],
        compiler_params=_params(("parallel", "parallel"), 48),
    )(proj, proj, proj, proj, proj, w2, b2, norm_g.reshape(1, C_DV))


def _mla_prep_kernel(cq_ref, ckv_ref, kr_ref, gq_ref, wq_ref, gkv_ref, wkv_ref, c_ref, sl_ref, sh_ref,
                     q_ref, k_ref, v_ref):
    half = MLA_ROPE // 2
    c, sl, sh = c_ref[...], sl_ref[...], sh_ref[...]

    def norm(x, g, rank):
        ms = jnp.sum(x * x, axis=-1, keepdims=True) * (1.0 / rank)
        return (x * lax.rsqrt(ms + EPS) * g).astype(BF16)

    qf = _dot(norm(cq_ref[...], gq_ref[...], MLA_Q_RANK), wq_ref[...])
    kvf = _dot(norm(ckv_ref[...], gkv_ref[...], MLA_KV_RANK), wkv_ref[...])
    k_rope = _rope3(kr_ref[...], c, sl, sh, half).astype(BF16)
    for h in range(D_HEADS):
        lo = h * MLA_HEAD_PAD
        q_ref[:, lo:lo + MLA_NOPE] = qf[:, lo:lo + MLA_NOPE].astype(BF16)
        q_ref[:, lo + MLA_NOPE:lo + MLA_HEAD_PAD] = _rope3(
            qf[:, lo + MLA_NOPE:lo + MLA_HEAD_PAD], c, sl, sh, half).astype(BF16)
        k_ref[:, lo:lo + MLA_NOPE] = kvf[:, lo:lo + MLA_NOPE].astype(BF16)
        k_ref[:, lo + MLA_NOPE:lo + MLA_HEAD_PAD] = k_rope
        v_ref[:, h * MLA_V:(h + 1) * MLA_V] = kvf[:, lo + MLA_NOPE:lo + MLA_HEAD_PAD].astype(BF16)


def _mla_prep(proj, gq, wq, gkv, wkv, tables, seq):
    t = proj.shape[0]
    tiles_per_seq = seq // ROW_TILE
    table = pl.BlockSpec((ROW_TILE, LANES), lambda i: (i % tiles_per_seq, 0))
    const = lambda shape: pl.BlockSpec(shape, lambda i: (0, 0))
    width = D_HEADS * MLA_HEAD_PAD
    return pl.pallas_call(
        _mla_prep_kernel,
        grid=(t // ROW_TILE,),
        in_specs=[
            pl.BlockSpec((ROW_TILE, 512), lambda i: (i, 3072 // 512)),
            pl.BlockSpec((ROW_TILE, 256), lambda i: (i, 3584 // 256)),
            pl.BlockSpec((ROW_TILE, 128), lambda i: (i, 3840 // 128)),
            const((1, 512)), const((512, width)), const((1, 256)), const((256, width)),
            table, table, table,
        ],
        out_specs=[
            pl.BlockSpec((ROW_TILE, width), lambda i: (i, 0)),
            pl.BlockSpec((ROW_TILE, width), lambda i: (i, 0)),
            pl.BlockSpec((ROW_TILE, D_HEADS * MLA_V), lambda i: (i, 0)),
        ],
        out_shape=[
            jax.ShapeDtypeStruct((t, width), BF16),
            jax.ShapeDtypeStruct((t, width), BF16),
            jax.ShapeDtypeStruct((t, D_HEADS * MLA_V), BF16),
        ],
        compiler_params=_params(("parallel",), 48),
    )(proj, proj, proj, gq, wq, gkv, wkv, *tables)


def _mla_attn_kernel(q_ref, k_ref, v_ref, o_ref):
    scale = (MLA_NOPE + MLA_ROPE) ** -0.5
    for sub in range(MLA_Q_TILE // MLA_Q_SUB):
        rows = slice(sub * MLA_Q_SUB, (sub + 1) * MLA_Q_SUB)
        sc = _dot_nt(q_ref[rows, :], k_ref[...]) * scale
        m = jnp.max(sc, axis=-1, keepdims=True)
        p = jnp.exp(sc - m)
        den = jnp.sum(p, axis=-1, keepdims=True)
        o_ref[rows, :] = _dot(p.astype(BF16), v_ref[...]) / den


def _mla_attention(q, k, v):
    b, s, _ = q.shape
    return pl.pallas_call(
        _mla_attn_kernel,
        grid=(b, D_HEADS, s // MLA_Q_TILE),
        in_specs=[
            pl.BlockSpec((None, MLA_Q_TILE, MLA_HEAD_PAD), lambda bi, hi, qi: (bi, qi, hi)),
            pl.BlockSpec((None, s, MLA_HEAD_PAD), lambda bi, hi, qi: (bi, 0, hi)),
            pl.BlockSpec((None, s, MLA_V), lambda bi, hi, qi: (bi, 0, hi)),
        ],
        out_specs=pl.BlockSpec((None, MLA_Q_TILE, MLA_V), lambda bi, hi, qi: (bi, qi, hi)),
        out_shape=jax.ShapeDtypeStruct((b, s, D_HEADS * MLA_V), F32),
        compiler_params=_params(("parallel", "parallel", "arbitrary"), 48),
    )(q, k, v)


def _offsets(sizes):
    out, acc = [], 0
    for sz in sizes[:-1]:
        acc += sz
        out.append(acc)
    return out


def _pad_cols(t, n):
    return jnp.pad(t, ((0, 0), (0, n - t.shape[1])))


def _odd_in_layout(w):
    qc, kc, vc, rc, af, ab, cq, ckv, kr = jnp.split(w, _offsets(ODD_SIZES), axis=1)
    cols = [qc, kc, vc, rc, _pad_cols(cq, 512), _pad_cols(ckv, 256), _pad_cols(kr, 128),
            _pad_cols(jnp.concatenate([af, ab], axis=1), 128)]
    out = jnp.concatenate(cols, axis=1)
    assert out.shape[1] == ODD_PAD_IN
    return out


def _gla_gate_layout(w2, bias):
    wf = w2[0].reshape(GLA_RANK, C_HEADS, C_DK).transpose(1, 0, 2)
    wb = w2[1].reshape(GLA_RANK, C_HEADS, C_DK).transpose(1, 0, 2)
    z = jnp.zeros_like(wf)
    top = jnp.concatenate([wf, z], axis=2)
    bot = jnp.concatenate([z, wb], axis=2)
    rest = jnp.zeros((C_HEADS, LANES - 2 * GLA_RANK, 2 * C_DK), w2.dtype)
    w = jnp.concatenate([top, bot, rest], axis=1)
    b = jnp.concatenate([bias[0].reshape(C_HEADS, 1, C_DK), bias[1].reshape(C_HEADS, 1, C_DK)], axis=2)
    return w.astype(BF16), b


def _mixer_even(h, g, w_in, w_out, ret_decay, b, s):
    proj = _norm_proj(h, g, w_in.astype(BF16)).reshape(b, s, -1)
    oa = _dilated_attention(proj, _rope_tables(s, ROPE_THETA, ROPE_DIM, HEAD_DIM))
    c, s_lo, s_hi = _rope_tables(s, RET_ROPE_THETA, HEAD_DIM, HEAD_DIM)
    ob = _retention(proj, ret_decay, (c, s_lo + s_hi))
    wo = w_out.astype(BF16)
    ka = A_HEADS * HEAD_DIM
    return _out_proj(h, oa.reshape(b * s, -1), ob.reshape(b * s, -1), wo[:ka], wo[ka:])


def _mixer_odd(h, g, w_in, w_out, gate_w2, gate_b, gla_norm_g, q_norm_g, w_uq, kv_norm_g, w_ukv, b, s):
    proj = _norm_proj(h, g, _odd_in_layout(w_in).astype(BF16))
    w2, b2 = _gla_gate_layout(gate_w2, gate_b)
    oc = _gla(proj.reshape(b, s, -1), w2, b2, gla_norm_g)
    wq = jnp.pad(w_uq.reshape(MLA_Q_RANK, D_HEADS, MLA_NOPE + MLA_ROPE),
                 ((0, 512 - MLA_Q_RANK), (0, 0), (0, MLA_HEAD_PAD - MLA_NOPE - MLA_ROPE)))
    wq = wq.reshape(512, D_HEADS * MLA_HEAD_PAD).astype(BF16)
    wkv = jnp.pad(w_ukv, ((0, 256 - MLA_KV_RANK), (0, 0))).astype(BF16)
    gq = jnp.pad(q_norm_g, (0, 512 - MLA_Q_RANK)).reshape(1, 512)
    gkv = jnp.pad(kv_norm_g, (0, 256 - MLA_KV_RANK)).reshape(1, 256)
    q, k, v = _mla_prep(proj, gq, wq, gkv, wkv, _rope_tables(s, MLA_ROPE_THETA, MLA_ROPE, LANES), s)
    od = _mla_attention(q.reshape(b, s, -1), k.reshape(b, s, -1), v.reshape(b, s, -1))
    wo = w_out.astype(BF16)
    kc = C_HEADS * C_DV
    return _out_proj(h, oc.reshape(b * s, -1), od.reshape(b * s, -1), wo[:kc], wo[kc:])


def kernel(x_prompt, x_sample, norm_g, final_norm_g, ffn_w_gate, ffn_w_up, ffn_w_down, ab_w_in, ab_w_out,
           ret_decay, cd_w_in, cd_w_out, gla_gate_w2, gla_gate_b, gla_norm_g, mla_q_norm_g, mla_w_uq,
           mla_kv_norm_g, mla_w_ukv):
    depth = norm_g.shape[0]
    w_gate, w_up, w_down = (w.astype(BF16) for w in (ffn_w_gate, ffn_w_up, ffn_w_down))

    def trunk(x):
        b, s, d = x.shape
        h = x.reshape(b * s, d)
        for i in range(depth):
            j = i // 2
            ffn = lambda hh, slot, fg=None: _ffn(hh, norm_g[i, 2 * slot], w_gate[i, slot], w_up[i, slot],
                                                w_down[i, slot], fg)
            h = ffn(h, 0)
            if i % 2 == 0:
                h = _mixer_even(h, norm_g[i, 1], ab_w_in[j], ab_w_out[j], ret_decay[j], b, s)
            else:
                h = _mixer_odd(h, norm_g[i, 1], cd_w_in[j], cd_w_out[j], gla_gate_w2[j], gla_gate_b[j],
                               gla_norm_g[j], mla_q_norm_g[j], mla_w_uq[j], mla_kv_norm_g[j], mla_w_ukv[j], b, s)
            h = ffn(h, 1, final_norm_g if i == depth - 1 else None)
        return h.reshape(b, s, d)

    return trunk(x_prompt), trunk(x_sample)
```

```python
import functools

import jax
import jax.numpy as jnp
from jax import lax
from jax.experimental import pallas as pl
from jax.experimental.pallas import tpu as pltpu

F32 = jnp.float32
BF16 = jnp.bfloat16

D_MODEL = 2048
D_FF = 5632
HEAD_DIM = 128
EPS = 1e-6
NEG = -1e30
ROPE_THETA = 500000.0
ROPE_DIM = HEAD_DIM // 4
A_HEADS = 8
A_BRANCHES = ((128, 1), (512, 4), (2048, 16))
B_HEADS = 8
RET_ROPE_THETA = 10000.0
C_HEADS = 4
C_DK = 128
C_DV = 256
GLA_RANK = 16
GLA_TAU = 16.0
D_HEADS = 8
MLA_Q_RANK = 448
MLA_KV_RANK = 160
MLA_NOPE = 128
MLA_ROPE = 64
MLA_V = 128
MLA_ROPE_THETA = 10000.0

EVEN_SIZES = (A_HEADS * HEAD_DIM,) * 3 + (B_HEADS * HEAD_DIM,) * 4
ODD_SIZES = (C_HEADS * C_DK, C_HEADS * C_DK, C_HEADS * C_DV, C_HEADS * C_DV, GLA_RANK, GLA_RANK,
             MLA_Q_RANK, MLA_KV_RANK, MLA_ROPE)

LANES = 128
VMEM_BYTES_V7X = 64 * 2 ** 20

ROW_TILE = 512
FF_TILE = 512
PROJ_TILE = 512
PROJ_ROW_TILE = 1024
CHUNK = 128
MIXER_UNROLL = 8
DIL_UNROLL = 8
DIL_HALF = 64
DIL_KEYS = 2 * CHUNK
MLA_Q_TILE = 1024
MLA_Q_SUB = 256
MLA_HEAD_PAD = 256
ODD_PAD_IN = 4096


def _params(semantics, vmem_mib):
    return pltpu.CompilerParams(dimension_semantics=semantics, vmem_limit_bytes=vmem_mib * 2 ** 20)


def _rms(x):
    return x * lax.rsqrt(jnp.mean(x * x, axis=-1, keepdims=True) + EPS)


def _dot(a, b):
    return jnp.dot(a, b, preferred_element_type=F32)


def _dot_nt(a, b):
    return lax.dot_general(a, b, (((1,), (1,)), ((), ())), preferred_element_type=F32)


def _dot_tn(a, b):
    return lax.dot_general(a, b, (((0,), (0,)), ((), ())), preferred_element_type=F32)


def _silu(x):
    return x * jax.nn.sigmoid(x)


def _ffn_kernel(h_ref, g_ref, wg_ref, wu_ref, wd_ref, fg_ref, o_ref, xn_ref, *, final_norm):
    j = pl.program_id(1)

    @pl.when(j == 0)
    def _():
        xn_ref[...] = (_rms(h_ref[...]) * g_ref[...]).astype(BF16)
        o_ref[...] = jnp.zeros_like(o_ref)

    xn = xn_ref[...]
    gate = _dot(xn, wg_ref[...])
    up = _dot(xn, wu_ref[...])
    act = (_silu(gate) * up).astype(BF16)
    o_ref[...] += _dot(act, wd_ref[...])

    @pl.when(j == pl.num_programs(1) - 1)
    def _():
        y = h_ref[...] + 0.5 * o_ref[...]
        if final_norm:
            y = _rms(y) * fg_ref[...]
        o_ref[...] = y


def _ffn(h, g, w_gate, w_up, w_down, layer, slot, final_g=None):
    t, d = h.shape
    f = w_gate.shape[-1]
    final_norm = final_g is not None
    fg = (final_g if final_norm else g).reshape(1, d)
    return pl.pallas_call(
        functools.partial(_ffn_kernel, final_norm=final_norm),
        grid=(t // ROW_TILE, f // FF_TILE),
        in_specs=[
            pl.BlockSpec((ROW_TILE, d), lambda i, j: (i, 0)),
            pl.BlockSpec((1, d), lambda i, j: (0, 0)),
            pl.BlockSpec((None, None, d, FF_TILE), lambda i, j: (layer, slot, 0, j)),
            pl.BlockSpec((None, None, d, FF_TILE), lambda i, j: (layer, slot, 0, j)),
            pl.BlockSpec((None, None, FF_TILE, d), lambda i, j: (layer, slot, j, 0)),
            pl.BlockSpec((1, d), lambda i, j: (0, 0)),
        ],
        out_specs=pl.BlockSpec((ROW_TILE, d), lambda i, j: (i, 0)),
        out_shape=jax.ShapeDtypeStruct((t, d), F32),
        scratch_shapes=[pltpu.VMEM((ROW_TILE, d), BF16)],
        compiler_params=_params(("parallel", "arbitrary"), 48),
    )(h, g.reshape(1, d), w_gate, w_up, w_down, fg)


def _norm_proj_kernel(h_ref, g_ref, w_ref, o_ref, xn_ref):
    @pl.when(pl.program_id(1) == 0)
    def _():
        xn_ref[...] = (_rms(h_ref[...]) * g_ref[...]).astype(BF16)

    o_ref[...] = _dot(xn_ref[...], w_ref[...])


def _norm_proj(h, g, w):
    t, d = h.shape
    n = w.shape[1]
    return pl.pallas_call(
        _norm_proj_kernel,
        grid=(t // PROJ_ROW_TILE, n // PROJ_TILE),
        in_specs=[
            pl.BlockSpec((PROJ_ROW_TILE, d), lambda i, j: (i, 0)),
            pl.BlockSpec((1, d), lambda i, j: (0, 0)),
            pl.BlockSpec((d, PROJ_TILE), lambda i, j: (0, j)),
        ],
        out_specs=pl.BlockSpec((PROJ_ROW_TILE, PROJ_TILE), lambda i, j: (i, j)),
        out_shape=jax.ShapeDtypeStruct((t, n), F32),
        scratch_shapes=[pltpu.VMEM((PROJ_ROW_TILE, d), BF16)],
        compiler_params=_params(("parallel", "arbitrary"), 48),
    )(h, g.reshape(1, d), w)


def _out_proj_kernel(h_ref, a_ref, b_ref, wa_ref, wb_ref, o_ref):
    o_ref[...] = (h_ref[...] + _dot(a_ref[...].astype(BF16), wa_ref[...])
                  + _dot(b_ref[...].astype(BF16), wb_ref[...]))


def _out_proj(h, a, b, wa, wb):
    t, d = h.shape
    ka, kb = a.shape[1], b.shape[1]
    return pl.pallas_call(
        _out_proj_kernel,
        grid=(t // ROW_TILE,),
        in_specs=[
            pl.BlockSpec((ROW_TILE, d), lambda i: (i, 0)),
            pl.BlockSpec((ROW_TILE, ka), lambda i: (i, 0)),
            pl.BlockSpec((ROW_TILE, kb), lambda i: (i, 0)),
            pl.BlockSpec((ka, d), lambda i: (0, 0)),
            pl.BlockSpec((kb, d), lambda i: (0, 0)),
        ],
        out_specs=pl.BlockSpec((ROW_TILE, d), lambda i: (i, 0)),
        out_shape=jax.ShapeDtypeStruct((t, d), F32),
        compiler_params=_params(("parallel",), 48),
    )(h, a, b, wa, wb)


def _rope_tables(s, theta, rot_dim, width):
    half = rot_dim // 2
    pos = jnp.arange(s, dtype=F32)
    inv = jnp.power(jnp.float32(theta), -jnp.arange(half, dtype=F32) * (2.0 / rot_dim))
    ang = pos[:, None] * inv[None, :]
    cos, sin = jnp.cos(ang), jnp.sin(ang)
    zeros = lambda n: jnp.zeros((s, n), F32)
    c = jnp.concatenate([cos, cos, jnp.ones((s, width - rot_dim), F32)], axis=1)
    s_lo = jnp.concatenate([zeros(half), sin, zeros(width - rot_dim)], axis=1)
    s_hi = jnp.concatenate([-sin, zeros(width - half)], axis=1)
    return c, s_lo, s_hi


def _rope3(x, c, s_lo, s_hi, half):
    w = x.shape[-1]
    return x * c + pltpu.roll(x, half, 1) * s_lo + pltpu.roll(x, w - half, 1) * s_hi


def _dilated_kernel(q_ref, k_ref, v_ref, c_ref, sl_ref, sh_ref, o_ref, qs_ref, ks_ref, m_ref, l_ref, *, seq):
    half = ROPE_DIM // 2
    n_chunks = seq // CHUNK
    scale = HEAD_DIM ** -0.5

    def rope_chunk(i, carry):
        rows = pl.ds(pl.multiple_of(i * CHUNK, CHUNK), CHUNK)
        c, sl, sh = c_ref[rows, :], sl_ref[rows, :], sh_ref[rows, :]
        qs_ref[rows, :] = _rope3(q_ref[rows, :], c, sl, sh, half)
        ks_ref[rows, :] = _rope3(k_ref[rows, :], c, sl, sh, half)
        return carry

    lax.fori_loop(0, n_chunks, rope_chunk, 0)

    qi_local = lax.broadcasted_iota(jnp.int32, (CHUNK, DIL_KEYS), 0)
    ki_local = lax.broadcasted_iota(jnp.int32, (CHUNK, DIL_KEYS), 1)

    def branch(dil, first):
        sub_len = seq // dil
        n_blk = sub_len // CHUNK

        def body(it, carry):
            r = it // n_blk
            q0 = (it % n_blk) * CHUNK
            k0 = jnp.clip(q0 - DIL_HALF, 0, sub_len - DIL_KEYS)
            if dil == 1:
                q_rows = pl.ds(pl.multiple_of(q0, CHUNK), CHUNK)
                k_rows = pl.ds(pl.multiple_of(k0, DIL_HALF), DIL_KEYS)
            else:
                q_rows = pl.ds(r + q0 * dil, CHUNK, stride=dil)
                k_rows = pl.ds(r + k0 * dil, DIL_KEYS, stride=dil)
            q = qs_ref[q_rows, :].astype(BF16)
            k = ks_ref[k_rows, :].astype(BF16)
            v = v_ref[k_rows, :].astype(BF16)
            sc = _dot_nt(q, k) * scale
            valid = jnp.abs((ki_local + k0) - (qi_local + q0)) <= DIL_HALF
            sc = jnp.where(valid, sc, NEG)
            m_blk = jnp.max(sc, axis=-1, keepdims=True)
            if first:
                m_new = jnp.broadcast_to(m_blk, (CHUNK, LANES))
            else:
                m_old = m_ref[q_rows, :]
                m_new = jnp.maximum(m_old, m_blk)
            p = jnp.exp(sc - jnp.concatenate([m_new, m_new], axis=1))
            l_new = jnp.sum(p, axis=-1, keepdims=True)
            o_new = _dot(p.astype(BF16), v)
            if first:
                l_new = jnp.broadcast_to(l_new, (CHUNK, LANES))
            else:
                alpha = jnp.exp(m_old - m_new)
                l_new = alpha * l_ref[q_rows, :] + l_new
                o_new = alpha * o_ref[q_rows, :] + o_new
            m_ref[q_rows, :] = m_new
            l_ref[q_rows, :] = l_new
            o_ref[q_rows, :] = o_new
            return carry

        lax.fori_loop(0, dil * n_blk, body, 0, unroll=DIL_UNROLL if dil < 16 else DIL_UNROLL // 2)

    for idx, (window, dil) in enumerate(A_BRANCHES):
        assert window // (2 * dil) == DIL_HALF
        branch(dil, idx == 0)

    def finish(i, carry):
        rows = pl.ds(pl.multiple_of(i * CHUNK, CHUNK), CHUNK)
        o_ref[rows, :] = o_ref[rows, :] / l_ref[rows, :]
        return carry

    lax.fori_loop(0, n_chunks, finish, 0)


def _dilated_attention(proj, tables):
    b, s, _ = proj.shape
    head = lambda off: pl.BlockSpec((None, s, HEAD_DIM), lambda bi, hi: (bi, 0, off + hi))
    table = pl.BlockSpec((s, LANES), lambda bi, hi: (0, 0))
    return pl.pallas_call(
        functools.partial(_dilated_kernel, seq=s),
        grid=(b, A_HEADS),
        in_specs=[head(0), head(A_HEADS), head(2 * A_HEADS), table, table, table],
        out_specs=pl.BlockSpec((None, s, HEAD_DIM), lambda bi, hi: (bi, 0, hi)),
        out_shape=jax.ShapeDtypeStruct((b, s, A_HEADS * HEAD_DIM), F32),
        scratch_shapes=[pltpu.VMEM((s, HEAD_DIM), F32) for _ in range(4)],
        compiler_params=_params(("parallel", "parallel"), 48),
    )(proj, proj, proj, *tables)


def _retention_kernel(dec_ref, q_ref, k_ref, v_ref, g_ref, c_ref, s_ref, o_ref,
                      kr_ref, sb_ref, sf_ref, sbc_ref, *, seq):
    hi = pl.program_id(1)
    n = seq // CHUNK
    scale = HEAD_DIM ** -0.5
    cf32 = float(CHUNK)

    lg_f = -jnp.exp(jnp.full((1, LANES), dec_ref[0, hi], F32))
    lg_b = -jnp.exp(jnp.full((1, LANES), dec_ref[1, hi], F32))
    ri = lax.broadcasted_iota(jnp.int32, (CHUNK, CHUNK), 0)
    ci = lax.broadcasted_iota(jnp.int32, (CHUNK, CHUNK), 1)
    rel = (ri - ci).astype(F32)
    decay = jnp.where(rel >= 0, jnp.exp(lg_f * jnp.maximum(rel, 0.0)), jnp.exp(lg_b * jnp.maximum(-rel, 0.0)))
    row = lax.broadcasted_iota(jnp.int32, (CHUNK, LANES), 0).astype(F32)
    q_dec_f = jnp.exp(lg_f * (row + 1.0))
    k_dec_f = jnp.exp(lg_f * (cf32 - 1.0 - row))
    q_dec_b = jnp.exp(lg_b * (cf32 - row))
    k_dec_b = jnp.exp(lg_b * row)
    chunk_dec_f = jnp.exp(lg_f * cf32)
    chunk_dec_b = jnp.exp(lg_b * cf32)

    def rope(x, rows):
        return x * c_ref[rows, :] + pltpu.roll(x, HEAD_DIM // 2, 1) * s_ref[rows, :]

    sbc_ref[...] = jnp.zeros_like(sbc_ref)

    def back(t, carry):
        c = n - 1 - t
        rows = pl.ds(pl.multiple_of(c * CHUNK, CHUNK), CHUNK)
        k = rope(k_ref[rows, :], rows) * scale
        kr_ref[rows, :] = k
        state = sbc_ref[...]
        sb_ref[c] = state.astype(BF16)
        kv = _dot_tn((k * k_dec_b).astype(BF16), v_ref[rows, :].astype(BF16))
        sbc_ref[...] = chunk_dec_b * state + kv
        return carry

    lax.fori_loop(0, n, back, 0, unroll=MIXER_UNROLL)

    sf_ref[...] = jnp.zeros_like(sf_ref)

    def fwd(c, carry):
        rows = pl.ds(pl.multiple_of(c * CHUNK, CHUNK), CHUNK)
        q = rope(q_ref[rows, :], rows)
        k = kr_ref[rows, :]
        v = v_ref[rows, :].astype(BF16)
        att = _dot_nt(q.astype(BF16), k.astype(BF16)) * decay
        out = _dot(att.astype(BF16), v)
        state_f = sf_ref[...]
        q_both = jnp.concatenate([q * q_dec_f, q * q_dec_b], axis=1).astype(BF16)
        s_both = jnp.concatenate([state_f.astype(BF16), sb_ref[c]], axis=0)
        out = out + _dot(q_both, s_both)
        sf_ref[...] = chunk_dec_f * state_f + _dot_tn((k * k_dec_f).astype(BF16), v)
        mu = jnp.mean(out, axis=-1, keepdims=True)
        dev = out - mu
        var = jnp.mean(dev * dev, axis=-1, keepdims=True)
        o_ref[rows, :] = dev * lax.rsqrt(var + EPS) * _silu(g_ref[rows, :])
        return carry

    lax.fori_loop(0, n, fwd, 0, unroll=MIXER_UNROLL)


def _retention(proj, ret_decay, tables):
    b, s, _ = proj.shape
    base = 3 * A_HEADS
    head = lambda off: pl.BlockSpec((None, s, HEAD_DIM), lambda bi, hi: (bi, 0, base + off + hi))
    table = pl.BlockSpec((s, LANES), lambda bi, hi: (0, 0))
    n = s // CHUNK
    return pl.pallas_call(
        functools.partial(_retention_kernel, seq=s),
        grid=(b, B_HEADS),
        in_specs=[pl.BlockSpec(memory_space=pltpu.SMEM),
                  head(0), head(B_HEADS), head(2 * B_HEADS), head(3 * B_HEADS), table, table],
        out_specs=pl.BlockSpec((None, s, HEAD_DIM), lambda bi, hi: (bi, 0, hi)),
        out_shape=jax.ShapeDtypeStruct((b, s, B_HEADS * HEAD_DIM), F32),
        scratch_shapes=[
            pltpu.VMEM((s, HEAD_DIM), F32),
            pltpu.VMEM((n, HEAD_DIM, HEAD_DIM), BF16),
            pltpu.VMEM((HEAD_DIM, HEAD_DIM), F32),
            pltpu.VMEM((HEAD_DIM, HEAD_DIM), F32),
        ],
        compiler_params=_params(("parallel", "parallel"), 48),
    )(ret_decay, proj, proj, proj, proj, *tables)


def _split3(x):
    hi = x.astype(BF16)
    r1 = x - hi.astype(F32)
    mid = r1.astype(BF16)
    lo = (r1 - mid.astype(F32)).astype(BF16)
    return hi, mid, lo


def _cumsum_dot(tri, x):
    w = x.shape[1]
    y = _dot(tri, jnp.concatenate(_split3(x), axis=1))
    return (y[:, :w] + y[:, w:2 * w]) + y[:, 2 * w:]


def _gla_kernel(q_ref, k_ref, v_ref, r_ref, low_ref, w2_ref, b2_ref, ng_ref, o_ref,
                la_ref, sb_ref, sf_ref, sbc_ref, *, seq):
    n = seq // CHUNK
    scale = C_DK ** -0.5
    mid = CHUNK // 2

    def gates(i, carry):
        rows = pl.ds(pl.multiple_of(i * ROW_TILE, ROW_TILE), ROW_TILE)
        z = _dot(low_ref[rows, :].astype(BF16), w2_ref[...]) + b2_ref[...]
        la_ref[rows, :] = (jnp.minimum(z, 0.0) - jnp.log1p(jnp.exp(-jnp.abs(z)))) * (1.0 / GLA_TAU)
        return carry

    lax.fori_loop(0, seq // ROW_TILE, gates, 0)

    ri = lax.broadcasted_iota(jnp.int32, (CHUNK, CHUNK), 0)
    ci = lax.broadcasted_iota(jnp.int32, (CHUNK, CHUNK), 1)
    lower = ci <= ri
    tri_prefix = jnp.where(lower, 1.0, 0.0).astype(BF16)
    tri_suffix = jnp.where(ci >= ri, 1.0, 0.0).astype(BF16)

    sbc_ref[...] = jnp.zeros_like(sbc_ref)

    def back(t, carry):
        c = n - 1 - t
        rows = pl.ds(pl.multiple_of(c * CHUNK, CHUNK), CHUNK)
        cb = _cumsum_dot(tri_suffix, la_ref[rows, C_DK:])
        cb0 = cb[0:1, :]
        kb = k_ref[rows, :] * jnp.exp(cb0 - cb)
        state = sbc_ref[...]
        sb_ref[c] = state.astype(BF16)
        sbc_ref[...] = state * jnp.exp(cb0) + _dot_tn(v_ref[rows, :].astype(BF16), kb.astype(BF16))
        return carry

    lax.fori_loop(0, n, back, 0, unroll=MIXER_UNROLL)

    sf_ref[...] = jnp.zeros_like(sf_ref)

    def fwd(c, carry):
        rows = pl.ds(pl.multiple_of(c * CHUNK, CHUNK), CHUNK)
        q = q_ref[rows, :] * scale
        k = k_ref[rows, :]
        v = v_ref[rows, :].astype(BF16)
        cf = _cumsum_dot(tri_prefix, la_ref[rows, :C_DK])
        cb = _cumsum_dot(tri_suffix, la_ref[rows, C_DK:])
        cfm = cf[mid - 1:mid, :]
        cbm = cb[mid:mid + 1, :]
        a_f = _dot_nt((q * jnp.exp(cf - cfm)).astype(BF16), (k * jnp.exp(cfm - cf)).astype(BF16))
        a_b = _dot_nt((q * jnp.exp(cb - cbm)).astype(BF16), (k * jnp.exp(cbm - cb)).astype(BF16))
        att = jnp.where(lower, a_f, a_b)
        out = _dot(att.astype(BF16), v)
        state_f = sf_ref[...]
        q_both = jnp.concatenate([q * jnp.exp(cf), q * jnp.exp(cb)], axis=1).astype(BF16)
        s_both = jnp.concatenate([state_f.astype(BF16), sb_ref[c]], axis=1)
        out = out + _dot_nt(q_both, s_both)
        cfl = cf[CHUNK - 1:CHUNK, :]
        sf_ref[...] = state_f * jnp.exp(cfl) + _dot_tn(v, (k * jnp.exp(cfl - cf)).astype(BF16))
        y = _rms(out) * ng_ref[...]
        o_ref[rows, :] = y * _silu(r_ref[rows, :])
        return carry

    lax.fori_loop(0, n, fwd, 0, unroll=MIXER_UNROLL)


def _gla(proj, w2, b2, norm_g):
    b, s, _ = proj.shape
    n = s // CHUNK
    return pl.pallas_call(
        functools.partial(_gla_kernel, seq=s),
        grid=(b, C_HEADS),
        in_specs=[
            pl.BlockSpec((None, s, C_DK), lambda bi, hi: (bi, 0, hi)),
            pl.BlockSpec((None, s, C_DK), lambda bi, hi: (bi, 0, C_HEADS + hi)),
            pl.BlockSpec((None, s, C_DV), lambda bi, hi: (bi, 0, C_HEADS + hi)),
            pl.BlockSpec((None, s, C_DV), lambda bi, hi: (bi, 0, 2 * C_HEADS + hi)),
            pl.BlockSpec((None, s, LANES), lambda bi, hi: (bi, 0, ODD_PAD_IN // LANES - 1)),
            pl.BlockSpec((None, LANES, 2 * C_DK), lambda bi, hi: (hi, 0, 0)),
            pl.BlockSpec((None, 1, 2 * C_DK), lambda bi, hi: (hi, 0, 0)),
            pl.BlockSpec((1, C_DV), lambda bi, hi: (0, 0)),
        ],
        out_specs=pl.BlockSpec((None, s, C_DV), lambda bi, hi: (bi, 0, hi)),
        out_shape=jax.ShapeDtypeStruct((b, s, C_HEADS * C_DV), F32),
        scratch_shapes=[
            pltpu.VMEM((s, 2 * C_DK), F32),
            pltpu.VMEM((n, C_DV, C_DK), BF16),
            pltpu.VMEM((C_DV, C_DK), F32),
            pltpu.VMEM((C_DV, C_DK), F32),
        ],
        compiler_params=_params(("parallel", "parallel"), 48),
    )(proj, proj, proj, proj, proj, w2, b2, norm_g.reshape(1, C_DV))


def _mla_prep_kernel(cq_ref, ckv_ref, kr_ref, gq_ref, wq_ref, gkv_ref, wkv_ref, c_ref, sl_ref, sh_ref,
                     q_ref, k_ref, v_ref):
    half = MLA_ROPE // 2
    c, sl, sh = c_ref[...], sl_ref[...], sh_ref[...]

    def norm(x, g, rank):
        ms = jnp.sum(x * x, axis=-1, keepdims=True) * (1.0 / rank)
        return (x * lax.rsqrt(ms + EPS) * g).astype(BF16)

    qf = _dot(norm(cq_ref[...], gq_ref[...], MLA_Q_RANK), wq_ref[...])
    kvf = _dot(norm(ckv_ref[...], gkv_ref[...], MLA_KV_RANK), wkv_ref[...])
    k_rope = _rope3(kr_ref[...], c, sl, sh, half).astype(BF16)
    for h in range(D_HEADS):
        lo = h * MLA_HEAD_PAD
        q_ref[:, lo:lo + MLA_NOPE] = qf[:, lo:lo + MLA_NOPE].astype(BF16)
        q_ref[:, lo + MLA_NOPE:lo + MLA_HEAD_PAD] = _rope3(
            qf[:, lo + MLA_NOPE:lo + MLA_HEAD_PAD], c, sl, sh, half).astype(BF16)
        k_ref[:, lo:lo + MLA_NOPE] = kvf[:, lo:lo + MLA_NOPE].astype(BF16)
        k_ref[:, lo + MLA_NOPE:lo + MLA_HEAD_PAD] = k_rope
        v_ref[:, h * MLA_V:(h + 1) * MLA_V] = kvf[:, lo + MLA_NOPE:lo + MLA_HEAD_PAD].astype(BF16)


def _mla_prep(proj, gq, wq, gkv, wkv, tables, seq):
    t = proj.shape[0]
    tiles_per_seq = seq // ROW_TILE
    table = pl.BlockSpec((ROW_TILE, LANES), lambda i: (i % tiles_per_seq, 0))
    const = lambda shape: pl.BlockSpec(shape, lambda i: (0, 0))
    width = D_HEADS * MLA_HEAD_PAD
    return pl.pallas_call(
        _mla_prep_kernel,
        grid=(t // ROW_TILE,),
        in_specs=[
            pl.BlockSpec((ROW_TILE, 512), lambda i: (i, 3072 // 512)),
            pl.BlockSpec((ROW_TILE, 256), lambda i: (i, 3584 // 256)),
            pl.BlockSpec((ROW_TILE, 128), lambda i: (i, 3840 // 128)),
            const((1, 512)), const((512, width)), const((1, 256)), const((256, width)),
            table, table, table,
        ],
        out_specs=[
            pl.BlockSpec((ROW_TILE, width), lambda i: (i, 0)),
            pl.BlockSpec((ROW_TILE, width), lambda i: (i, 0)),
            pl.BlockSpec((ROW_TILE, D_HEADS * MLA_V), lambda i: (i, 0)),
        ],
        out_shape=[
            jax.ShapeDtypeStruct((t, width), BF16),
            jax.ShapeDtypeStruct((t, width), BF16),
            jax.ShapeDtypeStruct((t, D_HEADS * MLA_V), BF16),
        ],
        compiler_params=_params(("parallel",), 48),
    )(proj, proj, proj, gq, wq, gkv, wkv, *tables)


def _mla_attn_kernel(q_ref, k_ref, v_ref, o_ref):
    scale = (MLA_NOPE + MLA_ROPE) ** -0.5
    for sub in range(MLA_Q_TILE // MLA_Q_SUB):
        rows = slice(sub * MLA_Q_SUB, (sub + 1) * MLA_Q_SUB)
        sc = _dot_nt(q_ref[rows, :], k_ref[...]) * scale
        m = jnp.max(sc, axis=-1, keepdims=True)
        p = jnp.exp(sc - m)
        den = jnp.sum(p, axis=-1, keepdims=True)
        o_ref[rows, :] = _dot(p.astype(BF16), v_ref[...]) / den


def _mla_attention(q, k, v):
    b, s, _ = q.shape
    return pl.pallas_call(
        _mla_attn_kernel,
        grid=(b, D_HEADS, s // MLA_Q_TILE),
        in_specs=[
            pl.BlockSpec((None, MLA_Q_TILE, MLA_HEAD_PAD), lambda bi, hi, qi: (bi, qi, hi)),
            pl.BlockSpec((None, s, MLA_HEAD_PAD), lambda bi, hi, qi: (bi, 0, hi)),
            pl.BlockSpec((None, s, MLA_V), lambda bi, hi, qi: (bi, 0, hi)),
        ],
        out_specs=pl.BlockSpec((None, MLA_Q_TILE, MLA_V), lambda bi, hi, qi: (bi, qi, hi)),
        out_shape=jax.ShapeDtypeStruct((b, s, D_HEADS * MLA_V), F32),
        compiler_params=_params(("parallel", "parallel", "arbitrary"), 48),
    )(q, k, v)


def _offsets(sizes):
    out, acc = [], 0
    for sz in sizes[:-1]:
        acc += sz
        out.append(acc)
    return out


def _pad_cols(t, n):
    return jnp.pad(t, ((0, 0), (0, n - t.shape[1])))


def _odd_in_layout(w):
    qc, kc, vc, rc, af, ab, cq, ckv, kr = jnp.split(w, _offsets(ODD_SIZES), axis=1)
    cols = [qc, kc, vc, rc, _pad_cols(cq, 512), _pad_cols(ckv, 256), _pad_cols(kr, 128),
            _pad_cols(jnp.concatenate([af, ab], axis=1), 128)]
    out = jnp.concatenate(cols, axis=1)
    assert out.shape[1] == ODD_PAD_IN
    return out


def _gla_gate_layout(w2, bias):
    wf = w2[0].reshape(GLA_RANK, C_HEADS, C_DK).transpose(1, 0, 2)
    wb = w2[1].reshape(GLA_RANK, C_HEADS, C_DK).transpose(1, 0, 2)
    z = jnp.zeros_like(wf)
    top = jnp.concatenate([wf, z], axis=2)
    bot = jnp.concatenate([z, wb], axis=2)
    rest = jnp.zeros((C_HEADS, LANES - 2 * GLA_RANK, 2 * C_DK), w2.dtype)
    w = jnp.concatenate([top, bot, rest], axis=1)
    b = jnp.concatenate([bias[0].reshape(C_HEADS, 1, C_DK), bias[1].reshape(C_HEADS, 1, C_DK)], axis=2)
    return w.astype(BF16), b


def _mixer_even(h, g, w_in, w_out, ret_decay, b, s):
    proj = _norm_proj(h, g, w_in.astype(BF16)).reshape(b, s, -1)
    oa = _dilated_attention(proj, _rope_tables(s, ROPE_THETA, ROPE_DIM, HEAD_DIM))
    c, s_lo, s_hi = _rope_tables(s, RET_ROPE_THETA, HEAD_DIM, HEAD_DIM)
    ob = _retention(proj, ret_decay, (c, s_lo + s_hi))
    wo = w_out.astype(BF16)
    ka = A_HEADS * HEAD_DIM
    return _out_proj(h, oa.reshape(b * s, -1), ob.reshape(b * s, -1), wo[:ka], wo[ka:])


def _mixer_odd(h, g, w_in, w_out, gate_w2, gate_b, gla_norm_g, q_norm_g, w_uq, kv_norm_g, w_ukv, b, s):
    proj = _norm_proj(h, g, _odd_in_layout(w_in).astype(BF16))
    w2, b2 = _gla_gate_layout(gate_w2, gate_b)
    oc = _gla(proj.reshape(b, s, -1), w2, b2, gla_norm_g)
    wq = jnp.pad(w_uq.reshape(MLA_Q_RANK, D_HEADS, MLA_NOPE + MLA_ROPE),
                 ((0, 512 - MLA_Q_RANK), (0, 0), (0, MLA_HEAD_PAD - MLA_NOPE - MLA_ROPE)))
    wq = wq.reshape(512, D_HEADS * MLA_HEAD_PAD).astype(BF16)
    wkv = jnp.pad(w_ukv, ((0, 256 - MLA_KV_RANK), (0, 0))).astype(BF16)
    gq = jnp.pad(q_norm_g, (0, 512 - MLA_Q_RANK)).reshape(1, 512)
    gkv = jnp.pad(kv_norm_g, (0, 256 - MLA_KV_RANK)).reshape(1, 256)
    q, k, v = _mla_prep(proj, gq, wq, gkv, wkv, _rope_tables(s, MLA_ROPE_THETA, MLA_ROPE, LANES), s)
    od = _mla_attention(q.reshape(b, s, -1), k.reshape(b, s, -1), v.reshape(b, s, -1))
    wo = w_out.astype(BF16)
    kc = C_HEADS * C_DV
    return _out_proj(h, oc.reshape(b * s, -1), od.reshape(b * s, -1), wo[:kc], wo[kc:])


def kernel(x_prompt, x_sample, norm_g, final_norm_g, ffn_w_gate, ffn_w_up, ffn_w_down, ab_w_in, ab_w_out,
           ret_decay, cd_w_in, cd_w_out, gla_gate_w2, gla_gate_b, gla_norm_g, mla_q_norm_g, mla_w_uq,
           mla_kv_norm_g, mla_w_ukv):
    depth = norm_g.shape[0]
    w_gate, w_up, w_down = (w.astype(BF16) for w in (ffn_w_gate, ffn_w_up, ffn_w_down))

    def trunk(x):
        b, s, d = x.shape
        h = x.reshape(b * s, d)
        for i in range(depth):
            j = i // 2
            ffn = lambda hh, slot, fg=None: _ffn(hh, norm_g[i, 2 * slot], w_gate, w_up, w_down, i, slot, fg)
            h = ffn(h, 0)
            if i % 2 == 0:
                h = _mixer_even(h, norm_g[i, 1], ab_w_in[j], ab_w_out[j], ret_decay[j], b, s)
            else:
                h = _mixer_odd(h, norm_g[i, 1], cd_w_in[j], cd_w_out[j], gla_gate_w2[j], gla_gate_b[j],
                               gla_norm_g[j], mla_q_norm_g[j], mla_w_uq[j], mla_kv_norm_g[j], mla_w_ukv[j], b, s)
            h = ffn(h, 1, final_norm_g if i == depth - 1 else None)
        return h.reshape(b, s, d)

    return trunk(x_prompt), trunk(x_sample)
```

```python
import functools

import jax
import jax.numpy as jnp
from jax import lax
from jax.experimental import pallas as pl
from jax.experimental.pallas import tpu as pltpu

F32 = jnp.float32
BF16 = jnp.bfloat16

D_MODEL = 2048
D_FF = 5632
HEAD_DIM = 128
EPS = 1e-6
NEG = -1e30
ROPE_THETA = 500000.0
ROPE_DIM = HEAD_DIM // 4
A_HEADS = 8
A_BRANCHES = ((128, 1), (512, 4), (2048, 16))
B_HEADS = 8
RET_ROPE_THETA = 10000.0
C_HEADS = 4
C_DK = 128
C_DV = 256
GLA_RANK = 16
GLA_TAU = 16.0
D_HEADS = 8
MLA_Q_RANK = 448
MLA_KV_RANK = 160
MLA_NOPE = 128
MLA_ROPE = 64
MLA_V = 128
MLA_ROPE_THETA = 10000.0

EVEN_SIZES = (A_HEADS * HEAD_DIM,) * 3 + (B_HEADS * HEAD_DIM,) * 4
ODD_SIZES = (C_HEADS * C_DK, C_HEADS * C_DK, C_HEADS * C_DV, C_HEADS * C_DV, GLA_RANK, GLA_RANK,
             MLA_Q_RANK, MLA_KV_RANK, MLA_ROPE)

LANES = 128
VMEM_BYTES_V7X = 64 * 2 ** 20

ROW_TILE = 512
FF_TILE = 512
PROJ_TILE = 1024
PROJ_ROW_TILE = 1024
CHUNK = 128
MIXER_UNROLL = 8
DIL_UNROLL = 8
DIL_HALF = 64
DIL_KEYS = 2 * CHUNK
MLA_Q_TILE = 1024
MLA_Q_SUB = 256
MLA_HEAD_PAD = 256
ODD_PAD_IN = 4096


def _params(semantics, vmem_mib):
    return pltpu.CompilerParams(dimension_semantics=semantics, vmem_limit_bytes=vmem_mib * 2 ** 20)


def _rms(x):
    return x * lax.rsqrt(jnp.mean(x * x, axis=-1, keepdims=True) + EPS)


def _dot(a, b):
    return jnp.dot(a, b, preferred_element_type=F32)


def _dot_nt(a, b):
    return lax.dot_general(a, b, (((1,), (1,)), ((), ())), preferred_element_type=F32)


def _dot_tn(a, b):
    return lax.dot_general(a, b, (((0,), (0,)), ((), ())), preferred_element_type=F32)


def _silu(x):
    return x * jax.nn.sigmoid(x)


def _ffn_kernel(h_ref, g_ref, wg_ref, wu_ref, wd_ref, fg_ref, o_ref, xn_ref, *, final_norm):
    j = pl.program_id(1)

    @pl.when(j == 0)
    def _():
        xn_ref[...] = (_rms(h_ref[...]) * g_ref[...]).astype(BF16)
        o_ref[...] = jnp.zeros_like(o_ref)

    xn = xn_ref[...]
    gate = _dot(xn, wg_ref[...])
    up = _dot(xn, wu_ref[...])
    act = (_silu(gate) * up).astype(BF16)
    o_ref[...] += _dot(act, wd_ref[...])

    @pl.when(j == pl.num_programs(1) - 1)
    def _():
        y = h_ref[...] + 0.5 * o_ref[...]
        if final_norm:
            y = _rms(y) * fg_ref[...]
        o_ref[...] = y


def _ffn(h, g, w_gate, w_up, w_down, layer, slot, final_g=None):
    t, d = h.shape
    f = w_gate.shape[-1]
    final_norm = final_g is not None
    fg = (final_g if final_norm else g).reshape(1, d)
    return pl.pallas_call(
        functools.partial(_ffn_kernel, final_norm=final_norm),
        grid=(t // ROW_TILE, f // FF_TILE),
        in_specs=[
            pl.BlockSpec((ROW_TILE, d), lambda i, j: (i, 0)),
            pl.BlockSpec((1, d), lambda i, j: (0, 0)),
            pl.BlockSpec((None, None, d, FF_TILE), lambda i, j: (layer, slot, 0, j)),
            pl.BlockSpec((None, None, d, FF_TILE), lambda i, j: (layer, slot, 0, j)),
            pl.BlockSpec((None, None, FF_TILE, d), lambda i, j: (layer, slot, j, 0)),
            pl.BlockSpec((1, d), lambda i, j: (0, 0)),
        ],
        out_specs=pl.BlockSpec((ROW_TILE, d), lambda i, j: (i, 0)),
        out_shape=jax.ShapeDtypeStruct((t, d), F32),
        scratch_shapes=[pltpu.VMEM((ROW_TILE, d), BF16)],
        compiler_params=_params(("parallel", "arbitrary"), 48),
    )(h, g.reshape(1, d), w_gate, w_up, w_down, fg)


def _norm_proj_kernel(h_ref, g_ref, w_ref, o_ref, xn_ref):
    @pl.when(pl.program_id(1) == 0)
    def _():
        xn_ref[...] = (_rms(h_ref[...]) * g_ref[...]).astype(BF16)

    o_ref[...] = _dot(xn_ref[...], w_ref[...])


def _norm_proj(h, g, w):
    t, d = h.shape
    n = w.shape[1]
    return pl.pallas_call(
        _norm_proj_kernel,
        grid=(t // PROJ_ROW_TILE, n // PROJ_TILE),
        in_specs=[
            pl.BlockSpec((PROJ_ROW_TILE, d), lambda i, j: (i, 0)),
            pl.BlockSpec((1, d), lambda i, j: (0, 0)),
            pl.BlockSpec((d, PROJ_TILE), lambda i, j: (0, j)),
        ],
        out_specs=pl.BlockSpec((PROJ_ROW_TILE, PROJ_TILE), lambda i, j: (i, j)),
        out_shape=jax.ShapeDtypeStruct((t, n), F32),
        scratch_shapes=[pltpu.VMEM((PROJ_ROW_TILE, d), BF16)],
        compiler_params=_params(("parallel", "arbitrary"), 48),
    )(h, g.reshape(1, d), w)


def _out_proj_kernel(h_ref, a_ref, b_ref, wa_ref, wb_ref, o_ref):
    o_ref[...] = (h_ref[...] + _dot(a_ref[...].astype(BF16), wa_ref[...])
                  + _dot(b_ref[...].astype(BF16), wb_ref[...]))


def _out_proj(h, a, b, wa, wb):
    t, d = h.shape
    ka, kb = a.shape[1], b.shape[1]
    return pl.pallas_call(
        _out_proj_kernel,
        grid=(t // ROW_TILE,),
        in_specs=[
            pl.BlockSpec((ROW_TILE, d), lambda i: (i, 0)),
            pl.BlockSpec((ROW_TILE, ka), lambda i: (i, 0)),
            pl.BlockSpec((ROW_TILE, kb), lambda i: (i, 0)),
            pl.BlockSpec((ka, d), lambda i: (0, 0)),
            pl.BlockSpec((kb, d), lambda i: (0, 0)),
        ],
        out_specs=pl.BlockSpec((ROW_TILE, d), lambda i: (i, 0)),
        out_shape=jax.ShapeDtypeStruct((t, d), F32),
        compiler_params=_params(("parallel",), 48),
    )(h, a, b, wa, wb)


def _rope_tables(s, theta, rot_dim, width):
    half = rot_dim // 2
    pos = jnp.arange(s, dtype=F32)
    inv = jnp.power(jnp.float32(theta), -jnp.arange(half, dtype=F32) * (2.0 / rot_dim))
    ang = pos[:, None] * inv[None, :]
    cos, sin = jnp.cos(ang), jnp.sin(ang)
    zeros = lambda n: jnp.zeros((s, n), F32)
    c = jnp.concatenate([cos, cos, jnp.ones((s, width - rot_dim), F32)], axis=1)
    s_lo = jnp.concatenate([zeros(half), sin, zeros(width - rot_dim)], axis=1)
    s_hi = jnp.concatenate([-sin, zeros(width - half)], axis=1)
    return c, s_lo, s_hi


def _rope3(x, c, s_lo, s_hi, half):
    w = x.shape[-1]
    return x * c + pltpu.roll(x, half, 1) * s_lo + pltpu.roll(x, w - half, 1) * s_hi


def _dilated_kernel(q_ref, k_ref, v_ref, c_ref, sl_ref, sh_ref, o_ref, qs_ref, ks_ref, m_ref, l_ref, *, seq):
    half = ROPE_DIM // 2
    n_chunks = seq // CHUNK
    scale = HEAD_DIM ** -0.5

    def rope_chunk(i, carry):
        rows = pl.ds(pl.multiple_of(i * CHUNK, CHUNK), CHUNK)
        c, sl, sh = c_ref[rows, :], sl_ref[rows, :], sh_ref[rows, :]
        qs_ref[rows, :] = _rope3(q_ref[rows, :], c, sl, sh, half)
        ks_ref[rows, :] = _rope3(k_ref[rows, :], c, sl, sh, half)
        return carry

    lax.fori_loop(0, n_chunks, rope_chunk, 0)

    qi_local = lax.broadcasted_iota(jnp.int32, (CHUNK, DIL_KEYS), 0)
    ki_local = lax.broadcasted_iota(jnp.int32, (CHUNK, DIL_KEYS), 1)

    def branch(dil, first):
        sub_len = seq // dil
        n_blk = sub_len // CHUNK

        def body(it, carry):
            r = it // n_blk
            q0 = (it % n_blk) * CHUNK
            k0 = jnp.clip(q0 - DIL_HALF, 0, sub_len - DIL_KEYS)
            if dil == 1:
                q_rows = pl.ds(pl.multiple_of(q0, CHUNK), CHUNK)
                k_rows = pl.ds(pl.multiple_of(k0, DIL_HALF), DIL_KEYS)
            else:
                q_rows = pl.ds(r + q0 * dil, CHUNK, stride=dil)
                k_rows = pl.ds(r + k0 * dil, DIL_KEYS, stride=dil)
            q = qs_ref[q_rows, :].astype(BF16)
            k = ks_ref[k_rows, :].astype(BF16)
            v = v_ref[k_rows, :].astype(BF16)
            sc = _dot_nt(q, k) * scale
            valid = jnp.abs((ki_local + k0) - (qi_local + q0)) <= DIL_HALF
            sc = jnp.where(valid, sc, NEG)
            m_blk = jnp.max(sc, axis=-1, keepdims=True)
            if first:
                m_new = jnp.broadcast_to(m_blk, (CHUNK, LANES))
            else:
                m_old = m_ref[q_rows, :]
                m_new = jnp.maximum(m_old, m_blk)
            p = jnp.exp(sc - jnp.concatenate([m_new, m_new], axis=1))
            l_new = jnp.sum(p, axis=-1, keepdims=True)
            o_new = _dot(p.astype(BF16), v)
            if first:
                l_new = jnp.broadcast_to(l_new, (CHUNK, LANES))
            else:
                alpha = jnp.exp(m_old - m_new)
                l_new = alpha * l_ref[q_rows, :] + l_new
                o_new = alpha * o_ref[q_rows, :] + o_new
            m_ref[q_rows, :] = m_new
            l_ref[q_rows, :] = l_new
            o_ref[q_rows, :] = o_new
            return carry

        lax.fori_loop(0, dil * n_blk, body, 0, unroll=DIL_UNROLL if dil < 16 else DIL_UNROLL // 2)

    for idx, (window, dil) in enumerate(A_BRANCHES):
        assert window // (2 * dil) == DIL_HALF
        branch(dil, idx == 0)

    def finish(i, carry):
        rows = pl.ds(pl.multiple_of(i * CHUNK, CHUNK), CHUNK)
        o_ref[rows, :] = o_ref[rows, :] / l_ref[rows, :]
        return carry

    lax.fori_loop(0, n_chunks, finish, 0)


def _dilated_attention(proj, tables):
    b, s, _ = proj.shape
    head = lambda off: pl.BlockSpec((None, s, HEAD_DIM), lambda bi, hi: (bi, 0, off + hi))
    table = pl.BlockSpec((s, LANES), lambda bi, hi: (0, 0))
    return pl.pallas_call(
        functools.partial(_dilated_kernel, seq=s),
        grid=(b, A_HEADS),
        in_specs=[head(0), head(A_HEADS), head(2 * A_HEADS), table, table, table],
        out_specs=pl.BlockSpec((None, s, HEAD_DIM), lambda bi, hi: (bi, 0, hi)),
        out_shape=jax.ShapeDtypeStruct((b, s, A_HEADS * HEAD_DIM), F32),
        scratch_shapes=[pltpu.VMEM((s, HEAD_DIM), F32) for _ in range(4)],
        compiler_params=_params(("parallel", "parallel"), 48),
    )(proj, proj, proj, *tables)


def _retention_kernel(dec_ref, q_ref, k_ref, v_ref, g_ref, c_ref, s_ref, o_ref,
                      kr_ref, sb_ref, sf_ref, sbc_ref, *, seq):
    hi = pl.program_id(1)
    n = seq // CHUNK
    scale = HEAD_DIM ** -0.5
    cf32 = float(CHUNK)

    lg_f = -jnp.exp(jnp.full((1, LANES), dec_ref[0, hi], F32))
    lg_b = -jnp.exp(jnp.full((1, LANES), dec_ref[1, hi], F32))
    ri = lax.broadcasted_iota(jnp.int32, (CHUNK, CHUNK), 0)
    ci = lax.broadcasted_iota(jnp.int32, (CHUNK, CHUNK), 1)
    rel = (ri - ci).astype(F32)
    decay = jnp.where(rel >= 0, jnp.exp(lg_f * jnp.maximum(rel, 0.0)), jnp.exp(lg_b * jnp.maximum(-rel, 0.0)))
    row = lax.broadcasted_iota(jnp.int32, (CHUNK, LANES), 0).astype(F32)
    q_dec_f = jnp.exp(lg_f * (row + 1.0))
    k_dec_f = jnp.exp(lg_f * (cf32 - 1.0 - row))
    q_dec_b = jnp.exp(lg_b * (cf32 - row))
    k_dec_b = jnp.exp(lg_b * row)
    chunk_dec_f = jnp.exp(lg_f * cf32)
    chunk_dec_b = jnp.exp(lg_b * cf32)

    def rope(x, rows):
        return x * c_ref[rows, :] + pltpu.roll(x, HEAD_DIM // 2, 1) * s_ref[rows, :]

    sbc_ref[...] = jnp.zeros_like(sbc_ref)

    def back(t, carry):
        c = n - 1 - t
        rows = pl.ds(pl.multiple_of(c * CHUNK, CHUNK), CHUNK)
        k = rope(k_ref[rows, :], rows) * scale
        kr_ref[rows, :] = k
        state = sbc_ref[...]
        sb_ref[c] = state.astype(BF16)
        kv = _dot_tn((k * k_dec_b).astype(BF16), v_ref[rows, :].astype(BF16))
        sbc_ref[...] = chunk_dec_b * state + kv
        return carry

    lax.fori_loop(0, n, back, 0, unroll=MIXER_UNROLL)

    sf_ref[...] = jnp.zeros_like(sf_ref)

    def fwd(c, carry):
        rows = pl.ds(pl.multiple_of(c * CHUNK, CHUNK), CHUNK)
        q = rope(q_ref[rows, :], rows)
        k = kr_ref[rows, :]
        v = v_ref[rows, :].astype(BF16)
        att = _dot_nt(q.astype(BF16), k.astype(BF16)) * decay
        out = _dot(att.astype(BF16), v)
        state_f = sf_ref[...]
        q_both = jnp.concatenate([q * q_dec_f, q * q_dec_b], axis=1).astype(BF16)
        s_both = jnp.concatenate([state_f.astype(BF16), sb_ref[c]], axis=0)
        out = out + _dot(q_both, s_both)
        sf_ref[...] = chunk_dec_f * state_f + _dot_tn((k * k_dec_f).astype(BF16), v)
        mu = jnp.mean(out, axis=-1, keepdims=True)
        dev = out - mu
        var = jnp.mean(dev * dev, axis=-1, keepdims=True)
        o_ref[rows, :] = dev * lax.rsqrt(var + EPS) * _silu(g_ref[rows, :])
        return carry

    lax.fori_loop(0, n, fwd, 0, unroll=MIXER_UNROLL)


def _retention(proj, ret_decay, tables):
    b, s, _ = proj.shape
    base = 3 * A_HEADS
    head = lambda off: pl.BlockSpec((None, s, HEAD_DIM), lambda bi, hi: (bi, 0, base + off + hi))
    table = pl.BlockSpec((s, LANES), lambda bi, hi: (0, 0))
    n = s // CHUNK
    return pl.pallas_call(
        functools.partial(_retention_kernel, seq=s),
        grid=(b, B_HEADS),
        in_specs=[pl.BlockSpec(memory_space=pltpu.SMEM),
                  head(0), head(B_HEADS), head(2 * B_HEADS), head(3 * B_HEADS), table, table],
        out_specs=pl.BlockSpec((None, s, HEAD_DIM), lambda bi, hi: (bi, 0, hi)),
        out_shape=jax.ShapeDtypeStruct((b, s, B_HEADS * HEAD_DIM), F32),
        scratch_shapes=[
            pltpu.VMEM((s, HEAD_DIM), F32),
            pltpu.VMEM((n, HEAD_DIM, HEAD_DIM), BF16),
            pltpu.VMEM((HEAD_DIM, HEAD_DIM), F32),
            pltpu.VMEM((HEAD_DIM, HEAD_DIM), F32),
        ],
        compiler_params=_params(("parallel", "parallel"), 48),
    )(ret_decay, proj, proj, proj, proj, *tables)


def _split3(x):
    hi = x.astype(BF16)
    r1 = x - hi.astype(F32)
    mid = r1.astype(BF16)
    lo = (r1 - mid.astype(F32)).astype(BF16)
    return hi, mid, lo


def _cumsum_dot(tri, x):
    w = x.shape[1]
    y = _dot(tri, jnp.concatenate(_split3(x), axis=1))
    return (y[:, :w] + y[:, w:2 * w]) + y[:, 2 * w:]


def _gla_kernel(q_ref, k_ref, v_ref, r_ref, low_ref, w2_ref, b2_ref, ng_ref, o_ref,
                la_ref, sb_ref, sf_ref, sbc_ref, *, seq):
    n = seq // CHUNK
    scale = C_DK ** -0.5
    mid = CHUNK // 2

    def gates(i, carry):
        rows = pl.ds(pl.multiple_of(i * ROW_TILE, ROW_TILE), ROW_TILE)
        z = _dot(low_ref[rows, :].astype(BF16), w2_ref[...]) + b2_ref[...]
        la_ref[rows, :] = (jnp.minimum(z, 0.0) - jnp.log1p(jnp.exp(-jnp.abs(z)))) * (1.0 / GLA_TAU)
        return carry

    lax.fori_loop(0, seq // ROW_TILE, gates, 0)

    ri = lax.broadcasted_iota(jnp.int32, (CHUNK, CHUNK), 0)
    ci = lax.broadcasted_iota(jnp.int32, (CHUNK, CHUNK), 1)
    lower = ci <= ri
    tri_prefix = jnp.where(lower, 1.0, 0.0).astype(BF16)
    tri_suffix = jnp.where(ci >= ri, 1.0, 0.0).astype(BF16)

    sbc_ref[...] = jnp.zeros_like(sbc_ref)

    def back(t, carry):
        c = n - 1 - t
        rows = pl.ds(pl.multiple_of(c * CHUNK, CHUNK), CHUNK)
        cb = _cumsum_dot(tri_suffix, la_ref[rows, C_DK:])
        cb0 = cb[0:1, :]
        kb = k_ref[rows, :] * jnp.exp(cb0 - cb)
        state = sbc_ref[...]
        sb_ref[c] = state.astype(BF16)
        sbc_ref[...] = state * jnp.exp(cb0) + _dot_tn(v_ref[rows, :].astype(BF16), kb.astype(BF16))
        return carry

    lax.fori_loop(0, n, back, 0, unroll=MIXER_UNROLL)

    sf_ref[...] = jnp.zeros_like(sf_ref)

    def fwd(c, carry):
        rows = pl.ds(pl.multiple_of(c * CHUNK, CHUNK), CHUNK)
        q = q_ref[rows, :] * scale
        k = k_ref[rows, :]
        v = v_ref[rows, :].astype(BF16)
        cf = _cumsum_dot(tri_prefix, la_ref[rows, :C_DK])
        cb = _cumsum_dot(tri_suffix, la_ref[rows, C_DK:])
        cfm = cf[mid - 1:mid, :]
        cbm = cb[mid:mid + 1, :]
        a_f = _dot_nt((q * jnp.exp(cf - cfm)).astype(BF16), (k * jnp.exp(cfm - cf)).astype(BF16))
        a_b = _dot_nt((q * jnp.exp(cb - cbm)).astype(BF16), (k * jnp.exp(cbm - cb)).astype(BF16))
        att = jnp.where(lower, a_f, a_b)
        out = _dot(att.astype(BF16), v)
        state_f = sf_ref[...]
        q_both = jnp.concatenate([q * jnp.exp(cf), q * jnp.exp(cb)], axis=1).astype(BF16)
        s_both = jnp.concatenate([state_f.astype(BF16), sb_ref[c]], axis=1)
        out = out + _dot_nt(q_both, s_both)
        cfl = cf[CHUNK - 1:CHUNK, :]
        sf_ref[...] = state_f * jnp.exp(cfl) + _dot_tn(v, (k * jnp.exp(cfl - cf)).astype(BF16))
        y = _rms(out) * ng_ref[...]
        o_ref[rows, :] = y * _silu(r_ref[rows, :])
        return carry

    lax.fori_loop(0, n, fwd, 0, unroll=MIXER_UNROLL)


def _gla(proj, w2, b2, norm_g):
    b, s, _ = proj.shape
    n = s // CHUNK
    return pl.pallas_call(
        functools.partial(_gla_kernel, seq=s),
        grid=(b, C_HEADS),
        in_specs=[
            pl.BlockSpec((None, s, C_DK), lambda bi, hi: (bi, 0, hi)),
            pl.BlockSpec((None, s, C_DK), lambda bi, hi: (bi, 0, C_HEADS + hi)),
            pl.BlockSpec((None, s, C_DV), lambda bi, hi: (bi, 0, C_HEADS + hi)),
            pl.BlockSpec((None, s, C_DV), lambda bi, hi: (bi, 0, 2 * C_HEADS + hi)),
            pl.BlockSpec((None, s, LANES), lambda bi, hi: (bi, 0, ODD_PAD_IN // LANES - 1)),
            pl.BlockSpec((None, LANES, 2 * C_DK), lambda bi, hi: (hi, 0, 0)),
            pl.BlockSpec((None, 1, 2 * C_DK), lambda bi, hi: (hi, 0, 0)),
            pl.BlockSpec((1, C_DV), lambda bi, hi: (0, 0)),
        ],
        out_specs=pl.BlockSpec((None, s, C_DV), lambda bi, hi: (bi, 0, hi)),
        out_shape=jax.ShapeDtypeStruct((b, s, C_HEADS * C_DV), F32),
        scratch_shapes=[
            pltpu.VMEM((s, 2 * C_DK), F32),
            pltpu.VMEM((n, C_DV, C_DK), BF16),
            pltpu.VMEM((C_DV, C_DK), F32),
            pltpu.VMEM((C_DV, C_DK), F32),
        ],
        compiler_params=_params(("parallel", "parallel"), 48),
    )(proj, proj, proj, proj, proj, w2, b2, norm_g.reshape(1, C_DV))


def _mla_prep_kernel(cq_ref, ckv_ref, kr_ref, gq_ref, wq_ref, gkv_ref, wkv_ref, c_ref, sl_ref, sh_ref,
                     q_ref, k_ref, v_ref):
    half = MLA_ROPE // 2
    c, sl, sh = c_ref[...], sl_ref[...], sh_ref[...]

    def norm(x, g, rank):
        ms = jnp.sum(x * x, axis=-1, keepdims=True) * (1.0 / rank)
        return (x * lax.rsqrt(ms + EPS) * g).astype(BF16)

    qf = _dot(norm(cq_ref[...], gq_ref[...], MLA_Q_RANK), wq_ref[...])
    kvf = _dot(norm(ckv_ref[...], gkv_ref[...], MLA_KV_RANK), wkv_ref[...])
    k_rope = _rope3(kr_ref[...], c, sl, sh, half).astype(BF16)
    for h in range(D_HEADS):
        lo = h * MLA_HEAD_PAD
        q_ref[:, lo:lo + MLA_NOPE] = qf[:, lo:lo + MLA_NOPE].astype(BF16)
        q_ref[:, lo + MLA_NOPE:lo + MLA_HEAD_PAD] = _rope3(
            qf[:, lo + MLA_NOPE:lo + MLA_HEAD_PAD], c, sl, sh, half).astype(BF16)
        k_ref[:, lo:lo + MLA_NOPE] = kvf[:, lo:lo + MLA_NOPE].astype(BF16)
        k_ref[:, lo + MLA_NOPE:lo + MLA_HEAD_PAD] = k_rope
        v_ref[:, h * MLA_V:(h + 1) * MLA_V] = kvf[:, lo + MLA_NOPE:lo + MLA_HEAD_PAD].astype(BF16)


def _mla_prep(proj, gq, wq, gkv, wkv, tables, seq):
    t = proj.shape[0]
    tiles_per_seq = seq // ROW_TILE
    table = pl.BlockSpec((ROW_TILE, LANES), lambda i: (i % tiles_per_seq, 0))
    const = lambda shape: pl.BlockSpec(shape, lambda i: (0, 0))
    width = D_HEADS * MLA_HEAD_PAD
    return pl.pallas_call(
        _mla_prep_kernel,
        grid=(t // ROW_TILE,),
        in_specs=[
            pl.BlockSpec((ROW_TILE, 512), lambda i: (i, 3072 // 512)),
            pl.BlockSpec((ROW_TILE, 256), lambda i: (i, 3584 // 256)),
            pl.BlockSpec((ROW_TILE, 128), lambda i: (i, 3840 // 128)),
            const((1, 512)), const((512, width)), const((1, 256)), const((256, width)),
            table, table, table,
        ],
        out_specs=[
            pl.BlockSpec((ROW_TILE, width), lambda i: (i, 0)),
            pl.BlockSpec((ROW_TILE, width), lambda i: (i, 0)),
            pl.BlockSpec((ROW_TILE, D_HEADS * MLA_V), lambda i: (i, 0)),
        ],
        out_shape=[
            jax.ShapeDtypeStruct((t, width), BF16),
            jax.ShapeDtypeStruct((t, width), BF16),
            jax.ShapeDtypeStruct((t, D_HEADS * MLA_V), BF16),
        ],
        compiler_params=_params(("parallel",), 48),
    )(proj, proj, proj, gq, wq, gkv, wkv, *tables)


def _mla_attn_kernel(q_ref, k_ref, v_ref, o_ref):
    scale = (MLA_NOPE + MLA_ROPE) ** -0.5
    for sub in range(MLA_Q_TILE // MLA_Q_SUB):
        rows = slice(sub * MLA_Q_SUB, (sub + 1) * MLA_Q_SUB)
        sc = _dot_nt(q_ref[rows, :], k_ref[...]) * scale
        m = jnp.max(sc, axis=-1, keepdims=True)
        p = jnp.exp(sc - m)
        den = jnp.sum(p, axis=-1, keepdims=True)
        o_ref[rows, :] = _dot(p.astype(BF16), v_ref[...]) / den


def _mla_attention(q, k, v):
    b, s, _ = q.shape
    return pl.pallas_call(
        _mla_attn_kernel,
        grid=(b, D_HEADS, s // MLA_Q_TILE),
        in_specs=[
            pl.BlockSpec((None, MLA_Q_TILE, MLA_HEAD_PAD), lambda bi, hi, qi: (bi, qi, hi)),
            pl.BlockSpec((None, s, MLA_HEAD_PAD), lambda bi, hi, qi: (bi, 0, hi)),
            pl.BlockSpec((None, s, MLA_V), lambda bi, hi, qi: (bi, 0, hi)),
        ],
        out_specs=pl.BlockSpec((None, MLA_Q_TILE, MLA_V), lambda bi, hi, qi: (bi, qi, hi)),
        out_shape=jax.ShapeDtypeStruct((b, s, D_HEADS * MLA_V), F32),
        compiler_params=_params(("parallel", "parallel", "arbitrary"), 48),
    )(q, k, v)


def _offsets(sizes):
    out, acc = [], 0
    for sz in sizes[:-1]:
        acc += sz
        out.append(acc)
    return out


def _pad_cols(t, n):
    return jnp.pad(t, ((0, 0), (0, n - t.shape[1])))


def _odd_in_layout(w):
    qc, kc, vc, rc, af, ab, cq, ckv, kr = jnp.split(w, _offsets(ODD_SIZES), axis=1)
    cols = [qc, kc, vc, rc, _pad_cols(cq, 512), _pad_cols(ckv, 256), _pad_cols(kr, 128),
            _pad_cols(jnp.concatenate([af, ab], axis=1), 128)]
    out = jnp.concatenate(cols, axis=1)
    assert out.shape[1] == ODD_PAD_IN
    return out


def _gla_gate_layout(w2, bias):
    wf = w2[0].reshape(GLA_RANK, C_HEADS, C_DK).transpose(1, 0, 2)
    wb = w2[1].reshape(GLA_RANK, C_HEADS, C_DK).transpose(1, 0, 2)
    z = jnp.zeros_like(wf)
    top = jnp.concatenate([wf, z], axis=2)
    bot = jnp.concatenate([z, wb], axis=2)
    rest = jnp.zeros((C_HEADS, LANES - 2 * GLA_RANK, 2 * C_DK), w2.dtype)
    w = jnp.concatenate([top, bot, rest], axis=1)
    b = jnp.concatenate([bias[0].reshape(C_HEADS, 1, C_DK), bias[1].reshape(C_HEADS, 1, C_DK)], axis=2)
    return w.astype(BF16), b


def _mixer_even(h, g, w_in, w_out, ret_decay, b, s):
    proj = _norm_proj(h, g, w_in.astype(BF16)).reshape(b, s, -1)
    oa = _dilated_attention(proj, _rope_tables(s, ROPE_THETA, ROPE_DIM, HEAD_DIM))
    c, s_lo, s_hi = _rope_tables(s, RET_ROPE_THETA, HEAD_DIM, HEAD_DIM)
    ob = _retention(proj, ret_decay, (c, s_lo + s_hi))
    wo = w_out.astype(BF16)
    ka = A_HEADS * HEAD_DIM
    return _out_proj(h, oa.reshape(b * s, -1), ob.reshape(b * s, -1), wo[:ka], wo[ka:])


def _mixer_odd(h, g, w_in, w_out, gate_w2, gate_b, gla_norm_g, q_norm_g, w_uq, kv_norm_g, w_ukv, b, s):
    proj = _norm_proj(h, g, _odd_in_layout(w_in).astype(BF16))
    w2, b2 = _gla_gate_layout(gate_w2, gate_b)
    oc = _gla(proj.reshape(b, s, -1), w2, b2, gla_norm_g)
    wq = jnp.pad(w_uq.reshape(MLA_Q_RANK, D_HEADS, MLA_NOPE + MLA_ROPE),
                 ((0, 512 - MLA_Q_RANK), (0, 0), (0, MLA_HEAD_PAD - MLA_NOPE - MLA_ROPE)))
    wq = wq.reshape(512, D_HEADS * MLA_HEAD_PAD).astype(BF16)
    wkv = jnp.pad(w_ukv, ((0, 256 - MLA_KV_RANK), (0, 0))).astype(BF16)
    gq = jnp.pad(q_norm_g, (0, 512 - MLA_Q_RANK)).reshape(1, 512)
    gkv = jnp.pad(kv_norm_g, (0, 256 - MLA_KV_RANK)).reshape(1, 256)
    q, k, v = _mla_prep(proj, gq, wq, gkv, wkv, _rope_tables(s, MLA_ROPE_THETA, MLA_ROPE, LANES), s)
    od = _mla_attention(q.reshape(b, s, -1), k.reshape(b, s, -1), v.reshape(b, s, -1))
    wo = w_out.astype(BF16)
    kc = C_HEADS * C_DV
    return _out_proj(h, oc.reshape(b * s, -1), od.reshape(b * s, -1), wo[:kc], wo[kc:])


def kernel(x_prompt, x_sample, norm_g, final_norm_g, ffn_w_gate, ffn_w_up, ffn_w_down, ab_w_in, ab_w_out,
           ret_decay, cd_w_in, cd_w_out, gla_gate_w2, gla_gate_b, gla_norm_g, mla_q_norm_g, mla_w_uq,
           mla_kv_norm_g, mla_w_ukv):
    depth = norm_g.shape[0]
    w_gate, w_up, w_down = (w.astype(BF16) for w in (ffn_w_gate, ffn_w_up, ffn_w_down))

    def trunk(x):
        b, s, d = x.shape
        h = x.reshape(b * s, d)
        for i in range(depth):
            j = i // 2
            ffn = lambda hh, slot, fg=None: _ffn(hh, norm_g[i, 2 * slot], w_gate, w_up, w_down, i, slot, fg)
            h = ffn(h, 0)
            if i % 2 == 0:
                h = _mixer_even(h, norm_g[i, 1], ab_w_in[j], ab_w_out[j], ret_decay[j], b, s)
            else:
                h = _mixer_odd(h, norm_g[i, 1], cd_w_in[j], cd_w_out[j], gla_gate_w2[j], gla_gate_b[j],
                               gla_norm_g[j], mla_q_norm_g[j], mla_w_uq[j], mla_kv_norm_g[j], mla_w_ukv[j], b, s)
            h = ffn(h, 1, final_norm_g if i == depth - 1 else None)
        return h.reshape(b, s, d)

    return trunk(x_prompt), trunk(x_sample)
```
